```python
import math
import jax, jax.numpy as jnp
from jax import lax
import numpy as np

D_MODEL = 1024
BATCH = 8
SEQ = 4096
DEPTH = 4

D_MIX = D_MODEL
ATT_HEADS = 4
ATT_QK_DIM = 64
ATT_V_DIM = 128
ATT_WIDTH = ATT_HEADS * ATT_V_DIM
ATT_QK_COLS = ATT_HEADS * 2 * ATT_QK_DIM
CONV_WIDTH = D_MIX // 4
CONV_K = 3
RWKV_HEAD = 64
RWKV_WIDTH = D_MIX // 4
RWKV_HEADS = RWKV_WIDTH // RWKV_HEAD
DECAY_LORA = 64
ICLR_LORA = 64
RWKV_SHIFT_COLS = 3 * RWKV_WIDTH + DECAY_LORA + ICLR_LORA
IN_COLS = 2 * ATT_QK_COLS + 2 * ATT_WIDTH + 4 * CONV_WIDTH + RWKV_SHIFT_COLS + RWKV_WIDTH
N_BUCKETS = 32
MAX_DISTANCE = 128
Q_BLOCK = 128
NEG_INF = -1e30
NORM_EPS = 1e-6
SUBLN_EPS = 1e-5
GN_EPS = 64e-5

kernel_name = 'hybrid_diffattn_shortconv_rwkv7'


def rmsnorm(x, g, eps=NORM_EPS):
    xf = x.astype(jnp.float32)
    y = xf * lax.rsqrt(jnp.mean(xf * xf, axis=-1, keepdims=True) + eps)
    return (y * g.astype(jnp.float32)).astype(x.dtype)


def split_cols(p, sizes):
    out, start = [], 0
    for n in sizes:
        out.append(p[..., start:start + n])
        start += n
    return out


def t5_causal_bucket(dist):
    n = jnp.maximum(dist, 0)
    max_exact = N_BUCKETS // 2
    nf = jnp.maximum(n, 1).astype(jnp.float32)
    large = max_exact + (jnp.log(nf / max_exact) / math.log(MAX_DISTANCE / max_exact)
                         * (N_BUCKETS - max_exact)).astype(jnp.int32)
    large = jnp.minimum(large, N_BUCKETS - 1)
    return jnp.where(n < max_exact, n, large)


def diff_attention(q, k, v, lam, lambda_init, subln_g, rel_bias):
    b, s, h = q.shape[0], q.shape[1], q.shape[2]
    nb = s // Q_BLOCK
    scale = ATT_QK_DIM ** -0.5
    qf = (q.astype(jnp.float32) * scale).reshape(b, nb, Q_BLOCK, h, 2, ATT_QK_DIM).swapaxes(0, 1)
    kf = k.astype(jnp.float32)
    vf = v.astype(jnp.float32)
    table = rel_bias.astype(jnp.float32)
    key_pos = jnp.arange(s)

    def block(args):
        qb, i = args
        q_pos = i * Q_BLOCK + jnp.arange(Q_BLOCK)
        dist = q_pos[:, None] - key_pos[None, :]
        bias = jnp.transpose(table[t5_causal_bucket(dist)], (2, 0, 1))
        logits = jnp.einsum('bqhmd,bkhmd->bhmqk', qb, kf) + bias[None, :, None]
        logits = jnp.where(dist >= 0, logits, NEG_INF)
        p = jax.nn.softmax(logits, axis=-1)
        a = p[:, :, 0] - lam * p[:, :, 1]
        return jnp.einsum('bhqk,bkhd->bqhd', a, vf)

    o = lax.map(block, (qf, jnp.arange(nb)))
    o = o.swapaxes(0, 1).reshape(b, s, h, ATT_V_DIM)
    o = rmsnorm(o, subln_g, eps=SUBLN_EPS) * (1.0 - lambda_init)
    return o.reshape(b, s, h * ATT_V_DIM)


def short_conv(bg, cg, hin, conv_w):
    u = cg * hin
    kern = conv_w[:, None, :].astype(u.dtype)
    y = lax.conv_general_dilated(u, kern, window_strides=(1,), padding=[(CONV_K - 1, 0)],
                                 dimension_numbers=('NWC', 'WIO', 'NWC'),
                                 feature_group_count=u.shape[-1])
    return bg * y


def rwkv7_time_mix(p, mu, w0, w_up, a0, a_up, k_k, k_a, r_k, lnx_g, lnx_b):
    b, s, _ = p.shape
    f32 = jnp.float32
    p = p.astype(f32)
    prev = jnp.pad(p, ((0, 0), (1, 0), (0, 0)))[:, :-1]
    p = p + (prev - p) * mu.astype(f32)
    r, k, v, wd, ad = split_cols(p, (RWKV_WIDTH, RWKV_WIDTH, RWKV_WIDTH, DECAY_LORA, ICLR_LORA))
    w = -jax.nn.softplus(-(w0.astype(f32) + jnp.tanh(wd) @ w_up.astype(f32))) - 0.5
    decay = jnp.exp(-jnp.exp(w))
    a = jax.nn.sigmoid(a0.astype(f32) + ad @ a_up.astype(f32))
    heads = lambda t: t.reshape(b, s, RWKV_HEADS, RWKV_HEAD)
    kk = heads(k * k_k.astype(f32))
    kk = kk / jnp.maximum(jnp.sqrt(jnp.sum(kk * kk, axis=-1, keepdims=True)), 1e-12)
    k = k * (1.0 + (a - 1.0) * k_a.astype(f32))
    rh, kh, vh, wh, ah = heads(r), heads(k), heads(v), heads(decay), heads(a)

    def step(state, inp):
        r_t, k_t, v_t, w_t, kk_t, a_t = inp
        sa = jnp.einsum('bhvk,bhk->bhv', state, kk_t)
        state = (state * w_t[:, :, None, :]
                 - sa[..., None] * (kk_t * a_t)[:, :, None, :]
                 + v_t[..., None] * k_t[:, :, None, :])
        y = jnp.einsum('bhvk,bhk->bhv', state, r_t)
        return state, y

    xs = tuple(t.swapaxes(0, 1) for t in (rh, kh, vh, wh, kk, ah))
    s0 = jnp.zeros((b, RWKV_HEADS, RWKV_HEAD, RWKV_HEAD), f32)
    _, y = lax.scan(step, s0, xs)
    y = y.swapaxes(0, 1)
    mean = jnp.mean(y, axis=-1, keepdims=True)
    var = jnp.mean(jnp.square(y - mean), axis=-1, keepdims=True)
    y = (y - mean) * lax.rsqrt(var + GN_EPS)
    y = y.reshape(b, s, RWKV_WIDTH) * lnx_g.astype(f32) + lnx_b.astype(f32)
    bonus = jnp.sum(rh * kh * r_k.astype(f32), axis=-1, keepdims=True) * vh
    return y + bonus.reshape(b, s, RWKV_WIDTH)


def setup_inputs(seed: int = 0) -> dict:
    key = jax.random.key(seed)
    ks = jax.random.split(key, 20)
    nrm = jax.random.normal
    return {
        'x': nrm(ks[0], (BATCH, SEQ, D_MODEL), jnp.float32),
        'norm_g': 1.0 + 0.02 * nrm(ks[1], (DEPTH, D_MODEL), jnp.float32),
        'w_in': nrm(ks[2], (DEPTH, D_MODEL, IN_COLS), jnp.float32) * D_MODEL ** -0.5,
        'w_out': nrm(ks[3], (DEPTH, D_MIX, D_MODEL), jnp.float32) * D_MIX ** -0.5,
        'final_norm_g': 1.0 + 0.02 * nrm(ks[4], (D_MODEL,), jnp.float32),
        'rel_bias': 0.2 * nrm(ks[5], (N_BUCKETS, ATT_HEADS), jnp.float32),
        'lam_qk': 0.1 * nrm(ks[6], (DEPTH, 4, ATT_QK_DIM), jnp.float32),
        'subln_g': 1.0 + 0.02 * nrm(ks[7], (DEPTH, ATT_V_DIM), jnp.float32),
        'conv_w': nrm(ks[8], (DEPTH, CONV_K, CONV_WIDTH), jnp.float32) * CONV_K ** -0.5,
        'rwkv_mu': jax.random.uniform(ks[9], (DEPTH, RWKV_SHIFT_COLS), jnp.float32),
        'w0': jax.random.uniform(ks[10], (DEPTH, RWKV_WIDTH), jnp.float32, minval=-5.0, maxval=1.0),
        'w_up': 0.5 * nrm(ks[11], (DEPTH, DECAY_LORA, RWKV_WIDTH), jnp.float32) * DECAY_LORA ** -0.5,
        'a0': 0.1 * nrm(ks[12], (DEPTH, RWKV_WIDTH), jnp.float32),
        'a_up': 0.5 * nrm(ks[13], (DEPTH, ICLR_LORA, RWKV_WIDTH), jnp.float32) * ICLR_LORA ** -0.5,
        'k_k': 0.85 + 0.02 * nrm(ks[14], (DEPTH, RWKV_WIDTH), jnp.float32),
        'k_a': 1.0 + 0.02 * nrm(ks[15], (DEPTH, RWKV_WIDTH), jnp.float32),
        'r_k': 0.1 * nrm(ks[16], (DEPTH, RWKV_HEADS, RWKV_HEAD), jnp.float32),
        'lnx_g': 1.0 + 0.02 * nrm(ks[17], (DEPTH, RWKV_WIDTH), jnp.float32),
        'lnx_b': 0.01 * nrm(ks[18], (DEPTH, RWKV_WIDTH), jnp.float32),
    }


def reference(x, norm_g, w_in, w_out, final_norm_g, rel_bias, lam_qk, subln_g, conv_w,
              rwkv_mu, w0, w_up, a0, a_up, k_k, k_a, r_k, lnx_g, lnx_b):
    b, s, _ = x.shape
    sizes = (ATT_QK_COLS, ATT_QK_COLS, ATT_WIDTH, ATT_WIDTH,
             CONV_WIDTH, CONV_WIDTH, CONV_WIDTH, CONV_WIDTH,
             RWKV_SHIFT_COLS, RWKV_WIDTH)
    for l in range(DEPTH):
        h = rmsnorm(x, norm_g[l])
        p = h @ w_in[l]
        q, k, v, z_att, cb, cc, ch, z_conv, rw_p, z_rwkv = split_cols(p, sizes)
        lambda_init = 0.8 - 0.6 * math.exp(-0.3 * l)
        lq = lam_qk[l].astype(jnp.float32)
        lam = jnp.exp(jnp.sum(lq[0] * lq[1])) - jnp.exp(jnp.sum(lq[2] * lq[3])) + lambda_init
        att = diff_attention(q.reshape(b, s, ATT_HEADS, 2, ATT_QK_DIM),
                             k.reshape(b, s, ATT_HEADS, 2, ATT_QK_DIM),
                             v.reshape(b, s, ATT_HEADS, ATT_V_DIM),
                             lam, lambda_init, subln_g[l], rel_bias).astype(x.dtype)
        cv = short_conv(cb, cc, ch, conv_w[l])
        rw = rwkv7_time_mix(rw_p, rwkv_mu[l], w0[l], w_up[l], a0[l], a_up[l], k_k[l], k_a[l],
                            r_k[l], lnx_g[l], lnx_b[l]).astype(x.dtype)
        mixed = jnp.concatenate([att * jax.nn.silu(z_att),
                                 cv * jax.nn.silu(z_conv),
                                 rw * jax.nn.silu(z_rwkv)], axis=-1)
        x = x + mixed @ w_out[l]
    return rmsnorm(x, final_norm_g)
```

```python
import functools
import math

import numpy as np
import jax
import jax.numpy as jnp
from jax import lax
from jax.experimental import pallas as pl
from jax.experimental.pallas import tpu as pltpu

F32 = jnp.float32
BF16 = jnp.bfloat16

D_MODEL = 1024
DEPTH = 4
ATT_HEADS = 4
ATT_QK_DIM = 64
ATT_V_DIM = 128
ATT_WIDTH = ATT_HEADS * ATT_V_DIM
ATT_QK_COLS = ATT_HEADS * 2 * ATT_QK_DIM
CONV_WIDTH = 256
CONV_K = 3
RWKV_HEAD = 64
RWKV_WIDTH = 256
RWKV_HEADS = 4
DECAY_LORA = 64
ICLR_LORA = 64
RWKV_SHIFT_COLS = 3 * RWKV_WIDTH + DECAY_LORA + ICLR_LORA
IN_COLS = 2 * ATT_QK_COLS + 2 * ATT_WIDTH + 4 * CONV_WIDTH + RWKV_SHIFT_COLS + RWKV_WIDTH
N_BUCKETS = 32
MAX_DISTANCE = 128
NEG_INF = -1e30
NORM_EPS = 1e-6
SUBLN_EPS = 1e-5
GN_EPS = 64e-5

QKV_COLS = 2 * ATT_QK_COLS + ATT_WIDTH
GATE_COLS = ATT_WIDTH + 4 * CONV_WIDTH + RWKV_WIDTH
_OFF_ZATT = QKV_COLS
_OFF_RW = _OFF_ZATT + ATT_WIDTH + 4 * CONV_WIDTH
_OFF_ZRW = _OFF_RW + RWKV_SHIFT_COLS

LANES = 128
SUBLANES = 8
VMEM_LIMIT = 56 * 1024 * 1024

ATT_TILE = 256
TOK_TILE = 512
SEQ_CHUNK = 64
V_PER_LANE = 4
V_GROUPS = RWKV_HEAD // V_PER_LANE
K_GROUPS = RWKV_HEAD // SUBLANES
VG_BATCH = 4


def _dot(a, b):
    return jnp.dot(a, b, preferred_element_type=F32)


def _dot_nt(a, b):
    return lax.dot_general(a, b, (((1,), (1,)), ((), ())), preferred_element_type=F32)


def _split2(x):
    hi = x.astype(BF16)
    lo = (x - hi.astype(F32)).astype(BF16)
    return hi, lo


def _dot_f32(a, b):
    ah, al = _split2(a)
    bh, bl = _split2(b)
    return _dot(ah, bh) + (_dot(ah, bl) + _dot(al, bh))


def _seg_sum(x, seg):
    hi = x.astype(BF16)
    r1 = x - hi.astype(F32)
    mid = r1.astype(BF16)
    lo = (r1 - mid.astype(F32)).astype(BF16)
    return _dot(hi, seg) + (_dot(mid, seg) + _dot(lo, seg))


def _sigmoid(x):
    return 1.0 / (1.0 + jnp.exp(-x))


def _bucket_tiles(tile):
    r = np.arange(tile)[:, None]
    c = np.arange(tile)[None, :]
    out = []
    for off in (0, tile):
        dist = r - c + off
        n = np.maximum(dist, 0)
        max_exact = N_BUCKETS // 2
        nf = np.maximum(n, 1).astype(np.float32)
        large = max_exact + (np.log(nf / np.float32(max_exact)) / np.float32(math.log(MAX_DISTANCE / max_exact))
                             * np.float32(N_BUCKETS - max_exact)).astype(np.int32)
        large = np.minimum(large, N_BUCKETS - 1)
        bucket = np.where(n < max_exact, n, large)
        out.append(np.where(dist >= 0, bucket, -1))
    return np.stack(out).astype(np.int32)


def _bias_kernel(tab_ref, bucket_ref, o_ref):
    h = pl.program_id(0)
    bk = bucket_ref[0]
    far = tab_ref[N_BUCKETS - 1, h]
    val = jnp.zeros(bk.shape, F32)
    for i in range(N_BUCKETS - 1):
        val = jnp.where(bk == i, tab_ref[i, h] - far, val)
    o_ref[0, 0] = jnp.where(bk < 0, NEG_INF, val)


def _bias_tiles(rel_bias, tile):
    buckets = jnp.asarray(_bucket_tiles(tile))
    return pl.pallas_call(
        _bias_kernel,
        grid=(ATT_HEADS, 2),
        in_specs=[pl.BlockSpec(memory_space=pltpu.SMEM),
                  pl.BlockSpec((1, tile, tile), lambda h, d: (d, 0, 0))],
        out_specs=pl.BlockSpec((1, 1, tile, tile), lambda h, d: (h, d, 0, 0)),
        out_shape=jax.ShapeDtypeStruct((ATT_HEADS, 2, tile, tile), F32),
        name="bias_tiles",
    )(rel_bias.astype(F32), buckets)


def _col_chunks(start, stop, width=512):
    c = start
    while c < stop:
        yield c, min(width, stop - c)
        c += width


def _in_proj_kernel(x_ref, g_ref, w_ref, qkv_ref, gate_ref, rw_ref):
    x = x_ref[...]
    ms = jnp.mean(x * x, axis=-1, keepdims=True)
    h = (x * lax.rsqrt(ms + NORM_EPS) * g_ref[...]).astype(BF16)
    for c, n in _col_chunks(0, QKV_COLS):
        qkv_ref[:, c:c + n] = _dot(h, w_ref[:, c:c + n]).astype(BF16)
    for c, n in _col_chunks(_OFF_ZATT, _OFF_RW):
        gate_ref[:, c - _OFF_ZATT:c - _OFF_ZATT + n] = _dot(h, w_ref[:, c:c + n])
    for c, n in _col_chunks(_OFF_RW, _OFF_ZRW):
        rw_ref[:, c - _OFF_RW:c - _OFF_RW + n] = _dot(h, w_ref[:, c:c + n])
    g0 = _OFF_RW - _OFF_ZATT
    gate_ref[:, g0:g0 + RWKV_WIDTH] = _dot(h, w_ref[:, _OFF_ZRW:IN_COLS])


def _in_proj(x2d, g, w_bf16):
    m = x2d.shape[0]
    tm = TOK_TILE
    return pl.pallas_call(
        _in_proj_kernel,
        grid=(m // tm,),
        in_specs=[pl.BlockSpec((tm, D_MODEL), lambda i: (i, 0)),
                  pl.BlockSpec((1, D_MODEL), lambda i: (0, 0)),
                  pl.BlockSpec((D_MODEL, IN_COLS), lambda i: (0, 0))],
        out_specs=[pl.BlockSpec((tm, QKV_COLS), lambda i: (i, 0)),
                   pl.BlockSpec((tm, GATE_COLS), lambda i: (i, 0)),
                   pl.BlockSpec((tm, RWKV_SHIFT_COLS), lambda i: (i, 0))],
        out_shape=[jax.ShapeDtypeStruct((m, QKV_COLS), BF16),
                   jax.ShapeDtypeStruct((m, GATE_COLS), F32),
                   jax.ShapeDtypeStruct((m, RWKV_SHIFT_COLS), F32)],
        compiler_params=pltpu.CompilerParams(dimension_semantics=("arbitrary",),
                                             vmem_limit_bytes=VMEM_LIMIT),
        name="in_proj",
    )(x2d, g.reshape(1, D_MODEL), w_bf16)


def _attn_kernel(q_ref, k_ref, v_ref, z_ref, bias_ref, lq_ref, g_ref, o_ref,
                 m_ref, l_ref, acc_ref, *, tile, lambda_init):
    i = pl.program_id(2)
    q = q_ref[0] * jnp.asarray(ATT_QK_DIM ** -0.5, BF16)
    lane = lax.broadcasted_iota(jnp.int32, q.shape, 1)
    zero = jnp.zeros_like(q)
    q_maps = (jnp.where(lane < ATT_QK_DIM, q, zero), jnp.where(lane >= ATT_QK_DIM, q, zero))

    m_ref[...] = jnp.full(m_ref.shape, NEG_INF, F32)
    l_ref[...] = jnp.zeros(l_ref.shape, F32)
    acc_ref[...] = jnp.zeros(acc_ref.shape, F32)
    reps = tile // LANES

    def update(j, bias):
        start = pl.multiple_of(j * tile, tile)
        kb = k_ref[0, pl.ds(start, tile), :]
        vb = v_ref[0, pl.ds(start, tile), :]
        for idx in range(2):
            s = _dot_nt(q_maps[idx], kb)
            if bias is not None:
                s = s + bias
            m_prev = m_ref[idx]
            m_next = jnp.maximum(m_prev, jnp.max(s, axis=1, keepdims=True))
            alpha = jnp.exp(m_prev - m_next)
            p = jnp.exp(s - jnp.concatenate([m_next] * reps, axis=1))
            l_ref[idx] = alpha * l_ref[idx] + jnp.sum(p, axis=1, keepdims=True)
            m_ref[idx] = m_next
            acc_ref[idx] = alpha * acc_ref[idx] + _dot(p.astype(BF16), vb)

    def far_body(j, carry):
        update(j, None)
        return carry

    lax.fori_loop(0, jnp.maximum(i - 1, 0), far_body, 0)

    @pl.when(i >= 1)
    def _():
        update(i - 1, bias_ref[0, 1])

    update(i, bias_ref[0, 0])

    lq = lq_ref[...]
    lam = (jnp.exp(jnp.sum(lq[0:1] * lq[1:2], axis=1, keepdims=True))
           - jnp.exp(jnp.sum(lq[2:3] * lq[3:4], axis=1, keepdims=True)) + lambda_init)
    o = acc_ref[0] / l_ref[0] - lam * (acc_ref[1] / l_ref[1])
    ms = jnp.mean(o * o, axis=-1, keepdims=True)
    o = o * lax.rsqrt(ms + SUBLN_EPS) * g_ref[...] * (1.0 - lambda_init)
    z = z_ref[0]
    o_ref[0] = (o * (z * _sigmoid(z))).astype(o_ref.dtype)


def _attention(qkv, gates, bias_tiles, lam_qk_l, subln_g_l, lambda_init):
    b, s, _ = qkv.shape
    t = bias_tiles.shape[-1]
    kern = functools.partial(_attn_kernel, tile=t, lambda_init=lambda_init)
    return pl.pallas_call(
        kern,
        grid=(b, ATT_HEADS, s // t),
        in_specs=[pl.BlockSpec((1, t, LANES), lambda bi, h, i: (bi, i, h)),
                  pl.BlockSpec((1, s, LANES), lambda bi, h, i: (bi, 0, ATT_HEADS + h)),
                  pl.BlockSpec((1, s, LANES), lambda bi, h, i: (bi, 0, 2 * ATT_HEADS + h)),
                  pl.BlockSpec((1, t, LANES), lambda bi, h, i: (bi, i, h)),
                  pl.BlockSpec((1, 2, t, t), lambda bi, h, i: (h, 0, 0, 0)),
                  pl.BlockSpec((4, ATT_QK_DIM), lambda bi, h, i: (0, 0)),
                  pl.BlockSpec((1, ATT_V_DIM), lambda bi, h, i: (0, 0))],
        out_specs=pl.BlockSpec((1, t, LANES), lambda bi, h, i: (bi, i, h)),
        out_shape=jax.ShapeDtypeStruct((b, s, ATT_WIDTH), BF16),
        scratch_shapes=[pltpu.VMEM((2, t, LANES), F32),
                        pltpu.VMEM((2, t, LANES), F32),
                        pltpu.VMEM((2, t, ATT_V_DIM), F32)],
        compiler_params=pltpu.CompilerParams(
            dimension_semantics=("arbitrary", "arbitrary", "arbitrary"),
            vmem_limit_bytes=VMEM_LIMIT),
        name="diff_attention",
    )(qkv, qkv, qkv, gates, bias_tiles, lam_qk_l, subln_g_l.reshape(1, ATT_V_DIM))


def _segment_ones():
    idx = np.arange(RWKV_WIDTH) // RWKV_HEAD
    return jnp.asarray((idx[:, None] == idx[None, :]).astype(np.float32), dtype=BF16)


def _rwkv_prep_kernel(p_ref, halo_ref, mu_ref, w0_ref, wup_ref, a0_ref, aup_ref, kkw_ref, ka_ref,
                      rk_ref, seg_ref, r_o, k_o, v_o, w_o, kk_o, b_o, bonus_o):
    i = pl.program_id(1)
    p = p_ref[0]
    last_prev = jnp.where(i > 0, halo_ref[0][SUBLANES - 1:SUBLANES, :], 0.0)
    row = lax.broadcasted_iota(jnp.int32, p.shape, 0)
    prev = jnp.where(row == 0, last_prev, pltpu.roll(p, 1, axis=0))
    p = p + (prev - p) * mu_ref[...]
    w3 = 3 * RWKV_WIDTH
    r = p[:, 0:RWKV_WIDTH]
    k = p[:, RWKV_WIDTH:2 * RWKV_WIDTH]
    v = p[:, 2 * RWKV_WIDTH:w3]
    wd = p[:, w3:w3 + DECAY_LORA]
    ad = p[:, w3 + DECAY_LORA:w3 + DECAY_LORA + ICLR_LORA]
    seg = seg_ref[...]

    x = -(w0_ref[...] + _dot_f32(jnp.tanh(wd), wup_ref[...]))
    softplus = jnp.maximum(x, 0.0) + jnp.log(1.0 + jnp.exp(-jnp.abs(x)))
    w = -softplus - 0.5
    decay = jnp.exp(-jnp.exp(w))
    a = _sigmoid(a0_ref[...] + _dot_f32(ad, aup_ref[...]))
    kk = k * kkw_ref[...]
    norm = jnp.sqrt(_seg_sum(kk * kk, seg))
    kk = kk / jnp.maximum(norm, 1e-12)
    k = k * (1.0 + (a - 1.0) * ka_ref[...])

    r_o[0] = r
    k_o[0] = k
    v_o[0] = v
    w_o[0] = decay
    kk_o[0] = kk
    b_o[0] = kk * a
    bonus_o[0] = _seg_sum(r * k * rk_ref[...], seg) * v


def _rwkv_prep(rw, mu, w0, w_up, a0, a_up, k_k, k_a, r_k):
    b, s, _ = rw.shape
    tm = min(TOK_TILE, s)
    row = lambda a, n: a.reshape(1, n).astype(F32)
    full = lambda shape: pl.BlockSpec(shape, lambda bi, i: (0,) * len(shape))
    out_spec = pl.BlockSpec((1, tm, RWKV_WIDTH), lambda bi, i: (bi, i, 0))
    out_sds = jax.ShapeDtypeStruct((b, s, RWKV_WIDTH), F32)
    halo_blocks = tm // SUBLANES
    return pl.pallas_call(
        _rwkv_prep_kernel,
        grid=(b, s // tm),
        in_specs=[pl.BlockSpec((1, tm, RWKV_SHIFT_COLS), lambda bi, i: (bi, i, 0)),
                  pl.BlockSpec((1, SUBLANES, RWKV_SHIFT_COLS),
                               lambda bi, i: (bi, jnp.maximum(i * halo_blocks - 1, 0), 0)),
                  full((1, RWKV_SHIFT_COLS)), full((1, RWKV_WIDTH)), full((DECAY_LORA, RWKV_WIDTH)),
                  full((1, RWKV_WIDTH)), full((ICLR_LORA, RWKV_WIDTH)), full((1, RWKV_WIDTH)),
                  full((1, RWKV_WIDTH)), full((1, RWKV_WIDTH)), full((RWKV_WIDTH, RWKV_WIDTH))],
        out_specs=[out_spec] * 7,
        out_shape=[out_sds] * 7,
        compiler_params=pltpu.CompilerParams(dimension_semantics=("arbitrary", "arbitrary"),
                                             vmem_limit_bytes=VMEM_LIMIT),
        name="rwkv_prep",
    )(rw, rw, row(mu, RWKV_SHIFT_COLS), row(w0, RWKV_WIDTH), w_up.astype(F32), row(a0, RWKV_WIDTH),
      a_up.astype(F32), row(k_k, RWKV_WIDTH), row(k_a, RWKV_WIDTH), row(r_k, RWKV_WIDTH), _segment_ones())


def _tree_sum(xs):
    xs = list(xs)
    while len(xs) > 1:
        xs = [xs[a] + xs[a + 1] for a in range(0, len(xs) - 1, 2)] + ([xs[-1]] if len(xs) % 2 else [])
    return xs[0]


def _sublane_allsum(x):
    for shift in (4, 2, 1):
        x = x + pltpu.roll(x, shift, axis=0)
    return x


def _rwkv_seq_kernel(r_ref, w_ref, k_ref, kk_ref, b_ref, v_ref, y_ref, s_ref, *, steps):
    @pl.when(pl.program_id(0) == 0)
    def _():
        s_ref[...] = jnp.zeros(s_ref.shape, F32)

    kslices = [slice(SUBLANES * kg, SUBLANES * (kg + 1)) for kg in range(K_GROUPS)]

    for g in range(V_GROUPS // VG_BATCH):
        groups = [VG_BATCH * g + a for a in range(VG_BATCH)]

        def body(t, state):
            kkp = [kk_ref[t, ks, :] for ks in kslices]
            wp = [w_ref[t, ks, :] for ks in kslices]
            bp = [b_ref[t, ks, :] for ks in kslices]
            kp = [k_ref[t, ks, :] for ks in kslices]
            rp = [r_ref[t, ks, :] for ks in kslices]
            new_state = []
            for a, vg in enumerate(groups):
                st = state[a]
                sa = _sublane_allsum(_tree_sum(st[kg] * kkp[kg] for kg in range(K_GROUPS)))
                vb = jnp.broadcast_to(v_ref[t, vg:vg + 1, :], (SUBLANES, LANES))
                st = tuple((st[kg] * wp[kg] - sa * bp[kg]) + vb * kp[kg] for kg in range(K_GROUPS))
                y = _sublane_allsum(_tree_sum(st[kg] * rp[kg] for kg in range(K_GROUPS)))
                y_ref[t, vg:vg + 1, :] = y[0:1]
                new_state.append(st)
            return tuple(new_state)

        init = tuple(tuple(s_ref[vg, ks, :] for ks in kslices) for vg in groups)
        final = lax.fori_loop(0, steps, body, init)
        for a, vg in enumerate(groups):
            for kg, ks in enumerate(kslices):
                s_ref[vg, ks, :] = final[a][kg]


def _rwkv_recurrence(r_c, w_c, k_c, kk_c, b_c, v_r):
    s = r_c.shape[0]
    tc = min(SEQ_CHUNK, s)
    col = pl.BlockSpec((tc, RWKV_HEAD, LANES), lambda i: (i, 0, 0))
    rowspec = pl.BlockSpec((tc, V_GROUPS, LANES), lambda i: (i, 0, 0))
    return pl.pallas_call(
        functools.partial(_rwkv_seq_kernel, steps=tc),
        grid=(s // tc,),
        in_specs=[col] * 5 + [rowspec],
        out_specs=rowspec,
        out_shape=jax.ShapeDtypeStruct((s, V_GROUPS, LANES), F32),
        scratch_shapes=[pltpu.VMEM((V_GROUPS, RWKV_HEAD, LANES), F32)],
        compiler_params=pltpu.CompilerParams(dimension_semantics=("arbitrary",),
                                             vmem_limit_bytes=VMEM_LIMIT),
        name="rwkv_recurrence",
    )(r_c, w_c, k_c, kk_c, b_c, v_r)


def _to_key_layout(x):
    b, s, _ = x.shape
    y = x.reshape(b, s, RWKV_HEADS, RWKV_HEAD).transpose(1, 3, 0, 2).reshape(s, RWKV_HEAD, b * RWKV_HEADS)
    return jnp.tile(y, (1, 1, V_PER_LANE))


def _to_value_layout(x):
    b, s, _ = x.shape
    y = x.reshape(b, s, RWKV_HEADS, V_GROUPS, V_PER_LANE).transpose(1, 3, 4, 0, 2)
    return y.reshape(s, V_GROUPS, V_PER_LANE * b * RWKV_HEADS)


def _from_value_layout(y, b):
    s = y.shape[0]
    x = y.reshape(s, V_GROUPS, V_PER_LANE, b, RWKV_HEADS).transpose(3, 0, 4, 1, 2)
    return x.reshape(b, s, RWKV_WIDTH)


def _shift_rows(u, halo_u, shift):
    rolled = pltpu.roll(u, shift, axis=0)
    head = jnp.where(lax.broadcasted_iota(jnp.int32, halo_u.shape, 0) < shift,
                     pltpu.roll(halo_u, shift, axis=0), rolled[0:SUBLANES])
    return jnp.concatenate([head, rolled[SUBLANES:]], axis=0)


def _out_kernel(x_ref, att_ref, gate_ref, halo_ref, y_ref, bonus_ref, cw_ref, lng_ref, lnb_ref, seg_ref,
                wout_ref, fg_ref, o_ref, *, final):
    i = pl.program_id(1)
    c0 = ATT_WIDTH
    cb = gate_ref[0, :, c0:c0 + CONV_WIDTH]
    cc = gate_ref[0, :, c0 + CONV_WIDTH:c0 + 2 * CONV_WIDTH]
    ch = gate_ref[0, :, c0 + 2 * CONV_WIDTH:c0 + 3 * CONV_WIDTH]
    zc = gate_ref[0, :, c0 + 3 * CONV_WIDTH:c0 + 4 * CONV_WIDTH]
    zr = gate_ref[0, :, c0 + 4 * CONV_WIDTH:c0 + 4 * CONV_WIDTH + RWKV_WIDTH]
    u = cc * ch
    halo_u = jnp.where(i > 0, halo_ref[0, :, c0 + CONV_WIDTH:c0 + 2 * CONV_WIDTH]
                       * halo_ref[0, :, c0 + 2 * CONV_WIDTH:c0 + 3 * CONV_WIDTH], 0.0)
    cw = cw_ref[...]
    conv = (cw[0:1] * _shift_rows(u, halo_u, 2) + cw[1:2] * _shift_rows(u, halo_u, 1)) + cw[2:3] * u
    cv = (cb * conv) * (zc * _sigmoid(zc))

    seg = seg_ref[...]
    y = y_ref[0]
    inv_n = 1.0 / RWKV_HEAD
    mean = _seg_sum(y, seg) * inv_n
    d = y - mean
    var = _seg_sum(d * d, seg) * inv_n
    yn = d * lax.rsqrt(var + GN_EPS) * lng_ref[...] + lnb_ref[...]
    rw = (yn + bonus_ref[0]) * (zr * _sigmoid(zr))

    upd = (_dot(att_ref[0], wout_ref[0:ATT_WIDTH, :])
           + _dot(cv.astype(BF16), wout_ref[ATT_WIDTH:ATT_WIDTH + CONV_WIDTH, :])
           + _dot(rw.astype(BF16), wout_ref[ATT_WIDTH + CONV_WIDTH:, :]))
    xn = x_ref[0] + upd
    if final:
        ms = jnp.mean(xn * xn, axis=-1, keepdims=True)
        xn = xn * lax.rsqrt(ms + NORM_EPS) * fg_ref[...]
    o_ref[0] = xn


def _out_proj(x, att, gates, y, bonus, conv_w_l, lnx_g_l, lnx_b_l, w_out_bf16, final_g, final):
    b, s, _ = x.shape
    tm = min(TOK_TILE, s)
    row = lambda a, n: a.reshape(1, n).astype(F32)
    full = lambda shape: pl.BlockSpec(shape, lambda bi, i: (0,) * len(shape))
    tok = lambda n: pl.BlockSpec((1, tm, n), lambda bi, i: (bi, i, 0))
    halo_blocks = tm // SUBLANES
    return pl.pallas_call(
        functools.partial(_out_kernel, final=final),
        grid=(b, s // tm),
        in_specs=[tok(D_MODEL), tok(ATT_WIDTH), tok(GATE_COLS),
                  pl.BlockSpec((1, SUBLANES, GATE_COLS),
                               lambda bi, i: (bi, jnp.maximum(i * halo_blocks - 1, 0), 0)),
                  tok(RWKV_WIDTH), tok(RWKV_WIDTH),
                  full((CONV_K, CONV_WIDTH)), full((1, RWKV_WIDTH)), full((1, RWKV_WIDTH)),
                  full((RWKV_WIDTH, RWKV_WIDTH)), full((D_MODEL, D_MODEL)), full((1, D_MODEL))],
        out_specs=tok(D_MODEL),
        out_shape=jax.ShapeDtypeStruct((b, s, D_MODEL), F32),
        compiler_params=pltpu.CompilerParams(dimension_semantics=("arbitrary", "arbitrary"),
                                             vmem_limit_bytes=VMEM_LIMIT),
        name="out_proj",
    )(x, att, gates, gates, y, bonus, conv_w_l.astype(F32),
      row(lnx_g_l, RWKV_WIDTH), row(lnx_b_l, RWKV_WIDTH), _segment_ones(), w_out_bf16,
      row(final_g, D_MODEL))


def kernel(x, norm_g, w_in, w_out, final_norm_g, rel_bias, lam_qk, subln_g, conv_w, rwkv_mu, w0, w_up,
           a0, a_up, k_k, k_a, r_k, lnx_g, lnx_b):
    b, s, _ = x.shape
    assert b * RWKV_HEADS * V_PER_LANE == LANES, "recurrence packs batch * heads * 4 values on the lanes"
    tile = min(ATT_TILE, s)
    bias = _bias_tiles(rel_bias, tile)
    w_in_b = w_in.astype(BF16)
    w_out_b = w_out.astype(BF16)
    x = x.astype(F32)
    for l in range(DEPTH):
        lambda_init = 0.8 - 0.6 * math.exp(-0.3 * l)
        qkv, gates, rw = _in_proj(x.reshape(b * s, D_MODEL), norm_g[l].astype(F32), w_in_b[l])
        qkv = qkv.reshape(b, s, QKV_COLS)
        gates = gates.reshape(b, s, GATE_COLS)
        rw = rw.reshape(b, s, RWKV_SHIFT_COLS)
        att = _attention(qkv, gates, bias, lam_qk[l].astype(F32), subln_g[l].astype(F32), lambda_init)
        r, k, v, decay, kk, bb, bonus = _rwkv_prep(rw, rwkv_mu[l], w0[l], w_up[l], a0[l], a_up[l],
                                                   k_k[l], k_a[l], r_k[l])
        y = _rwkv_recurrence(_to_key_layout(r), _to_key_layout(decay), _to_key_layout(k),
                             _to_key_layout(kk), _to_key_layout(bb), _to_value_layout(v))
        y = _from_value_layout(y, b)
        x = _out_proj(x, att, gates, y, bonus, conv_w[l], lnx_g[l], lnx_b[l], w_out_b[l],
                      final_norm_g, final=(l == DEPTH - 1))
    return x
```

```python
import functools
import math

import numpy as np
import jax
import jax.numpy as jnp
from jax import lax
from jax.experimental import pallas as pl
from jax.experimental.pallas import tpu as pltpu

F32 = jnp.float32
BF16 = jnp.bfloat16

D_MODEL = 1024
DEPTH = 4
ATT_HEADS = 4
ATT_QK_DIM = 64
ATT_V_DIM = 128
ATT_WIDTH = ATT_HEADS * ATT_V_DIM
ATT_QK_COLS = ATT_HEADS * 2 * ATT_QK_DIM
CONV_WIDTH = 256
CONV_K = 3
RWKV_HEAD = 64
RWKV_WIDTH = 256
RWKV_HEADS = 4
DECAY_LORA = 64
ICLR_LORA = 64
RWKV_SHIFT_COLS = 3 * RWKV_WIDTH + DECAY_LORA + ICLR_LORA
IN_COLS = 2 * ATT_QK_COLS + 2 * ATT_WIDTH + 4 * CONV_WIDTH + RWKV_SHIFT_COLS + RWKV_WIDTH
N_BUCKETS = 32
MAX_DISTANCE = 128
NEG_INF = -1e30
NORM_EPS = 1e-6
SUBLN_EPS = 1e-5
GN_EPS = 64e-5

QKV_COLS = 2 * ATT_QK_COLS + ATT_WIDTH
GATE_COLS = ATT_WIDTH + 4 * CONV_WIDTH + RWKV_WIDTH
_OFF_ZATT = QKV_COLS
_OFF_RW = _OFF_ZATT + ATT_WIDTH + 4 * CONV_WIDTH
_OFF_ZRW = _OFF_RW + RWKV_SHIFT_COLS

LOG2E = math.log2(math.e)
Q_SCALE = ATT_QK_DIM ** -0.5 * LOG2E

LANES = 128
SUBLANES = 8
VMEM_LIMIT = 56 * 1024 * 1024

ATT_TILE = 256
SCORE_LOOKAHEAD = 3
SUM_ROWS = 16
TOK_TILE = 512
SEQ_CHUNK = 64
V_PER_LANE = 4
V_GROUPS = RWKV_HEAD // V_PER_LANE
K_GROUPS = RWKV_HEAD // SUBLANES
VG_BATCH = 4


def _dot(a, b):
    return jnp.dot(a, b, preferred_element_type=F32)


def _dot_nt(a, b):
    return lax.dot_general(a, b, (((1,), (1,)), ((), ())), preferred_element_type=F32)


def _split2(x):
    hi = x.astype(BF16)
    lo = (x - hi.astype(F32)).astype(BF16)
    return hi, lo


def _dot_f32(a, b):
    ah, al = _split2(a)
    bh, bl = _split2(b)
    return _dot(ah, bh) + (_dot(ah, bl) + _dot(al, bh))


def _seg_sum(x, seg):
    hi = x.astype(BF16)
    r1 = x - hi.astype(F32)
    mid = r1.astype(BF16)
    lo = (r1 - mid.astype(F32)).astype(BF16)
    return _dot(hi, seg) + (_dot(mid, seg) + _dot(lo, seg))


def _sigmoid(x):
    return 1.0 / (1.0 + jnp.exp(-x))


def _bucket_tiles(tile):
    key = np.arange(tile)[:, None]
    query = np.arange(tile)[None, :]
    out = []
    for off in (0, tile):
        dist = query - key + off
        n = np.maximum(dist, 0)
        max_exact = N_BUCKETS // 2
        nf = np.maximum(n, 1).astype(np.float32)
        large = max_exact + (np.log(nf / np.float32(max_exact)) / np.float32(math.log(MAX_DISTANCE / max_exact))
                             * np.float32(N_BUCKETS - max_exact)).astype(np.int32)
        large = np.minimum(large, N_BUCKETS - 1)
        bucket = np.where(n < max_exact, n, large)
        out.append(np.where(dist >= 0, bucket, -1))
    return np.stack(out).astype(np.int32)


def _bias_kernel(tab_ref, bucket_ref, o_ref):
    h = pl.program_id(0)
    bk = bucket_ref[0]
    far = tab_ref[N_BUCKETS - 1, h]
    val = jnp.zeros(bk.shape, F32)
    for i in range(N_BUCKETS - 1):
        val = jnp.where(bk == i, (tab_ref[i, h] - far) * LOG2E, val)
    o_ref[0, 0] = jnp.where(bk < 0, NEG_INF, val)


def _bias_tiles(rel_bias, tile):
    buckets = jnp.asarray(_bucket_tiles(tile))
    return pl.pallas_call(
        _bias_kernel,
        grid=(ATT_HEADS, 2),
        in_specs=[pl.BlockSpec(memory_space=pltpu.SMEM),
                  pl.BlockSpec((1, tile, tile), lambda h, d: (d, 0, 0))],
        out_specs=pl.BlockSpec((1, 1, tile, tile), lambda h, d: (h, d, 0, 0)),
        out_shape=jax.ShapeDtypeStruct((ATT_HEADS, 2, tile, tile), F32),
        name="bias_tiles",
    )(rel_bias.astype(F32), buckets)


def _col_chunks(start, stop, width=512):
    c = start
    while c < stop:
        yield c, min(width, stop - c)
        c += width


def _in_proj_kernel(x_ref, g_ref, w_ref, qkv_ref, gate_ref, rw_ref):
    x = x_ref[...]
    ms = jnp.mean(x * x, axis=-1, keepdims=True)
    h = (x * lax.rsqrt(ms + NORM_EPS) * g_ref[...]).astype(BF16)
    for c, n in _col_chunks(0, QKV_COLS, ATT_QK_COLS):
        acc = _dot(h, w_ref[:, c:c + n])
        if c < ATT_QK_COLS:
            acc = acc * Q_SCALE
        qkv_ref[:, c:c + n] = acc.astype(BF16)
    for c, n in _col_chunks(_OFF_ZATT, _OFF_RW):
        gate_ref[:, c - _OFF_ZATT:c - _OFF_ZATT + n] = _dot(h, w_ref[:, c:c + n])
    for c, n in _col_chunks(_OFF_RW, _OFF_ZRW):
        rw_ref[:, c - _OFF_RW:c - _OFF_RW + n] = _dot(h, w_ref[:, c:c + n])
    g0 = _OFF_RW - _OFF_ZATT
    gate_ref[:, g0:g0 + RWKV_WIDTH] = _dot(h, w_ref[:, _OFF_ZRW:IN_COLS])


def _in_proj(x2d, g, w_bf16):
    m = x2d.shape[0]
    tm = TOK_TILE
    return pl.pallas_call(
        _in_proj_kernel,
        grid=(m // tm,),
        in_specs=[pl.BlockSpec((tm, D_MODEL), lambda i: (i, 0)),
                  pl.BlockSpec((1, D_MODEL), lambda i: (0, 0)),
                  pl.BlockSpec((D_MODEL, IN_COLS), lambda i: (0, 0))],
        out_specs=[pl.BlockSpec((tm, QKV_COLS), lambda i: (i, 0)),
                   pl.BlockSpec((tm, GATE_COLS), lambda i: (i, 0)),
                   pl.BlockSpec((tm, RWKV_SHIFT_COLS), lambda i: (i, 0))],
        out_shape=[jax.ShapeDtypeStruct((m, QKV_COLS), BF16),
                   jax.ShapeDtypeStruct((m, GATE_COLS), F32),
                   jax.ShapeDtypeStruct((m, RWKV_SHIFT_COLS), F32)],
        compiler_params=pltpu.CompilerParams(dimension_semantics=("arbitrary",),
                                             vmem_limit_bytes=VMEM_LIMIT),
        name="in_proj",
    )(x2d, g.reshape(1, D_MODEL), w_bf16)


def _attn_kernel(q_ref, k_ref, v_ref, z_ref, bias_ref, lq_ref, g_ref, o_ref,
                 qz_ref, vt_ref, m_ref, acc_ref, *, tile, seq, lambda_init):
    i = pl.program_id(1)
    chains = 2 * ATT_HEADS
    dv = ATT_V_DIM

    @pl.when(i == 0)
    def _():
        def tr(c, carry):
            start = pl.multiple_of(c * tile, tile)
            for h in range(ATT_HEADS):
                blk = v_ref[0, pl.ds(start, tile), h * LANES:(h + 1) * LANES]
                vt_ref[h, 0:dv, pl.ds(start, tile)] = blk.astype(F32).T.astype(BF16)
                vt_ref[h, dv:dv + SUM_ROWS, pl.ds(start, tile)] = jnp.ones((SUM_ROWS, tile), BF16)
            return carry
        lax.fori_loop(0, seq // tile, tr, 0)

    q = q_ref[0]
    lane = lax.broadcasted_iota(jnp.int32, (tile, LANES), 1)
    for h in range(ATT_HEADS):
        qh = q[:, h * LANES:(h + 1) * LANES]
        qz_ref[2 * h] = jnp.where(lane < ATT_QK_DIM, qh, jnp.zeros_like(qh))
        qz_ref[2 * h + 1] = jnp.where(lane >= ATT_QK_DIM, qh, jnp.zeros_like(qh))

    m_ref[...] = jnp.full(m_ref.shape, NEG_INF, F32)
    acc_ref[...] = jnp.zeros(acc_ref.shape, F32)

    def update(jobs):
        steps = [(pl.multiple_of(j * tile, tile), bias_idx, c) for j, bias_idx in jobs for c in range(chains)]

        def scores(start, bias_idx, c):
            h = c // 2
            kb = k_ref[0, pl.ds(start, tile), h * LANES:(h + 1) * LANES]
            s = _dot_nt(kb, qz_ref[c])
            if bias_idx is not None:
                s = s + bias_ref[h, bias_idx]
            return s

        def softmax(start, c, s):
            m_prev = m_ref[c:c + 1, :]
            m_next = jnp.maximum(m_prev, jnp.max(s, axis=0, keepdims=True))
            alpha = jnp.exp2(m_prev - m_next)
            p = jnp.exp2(s - m_next)
            m_ref[c:c + 1, :] = m_next
            vtb = vt_ref[c // 2, :, pl.ds(start, tile)]
            return alpha, _dot(vtb, p.astype(BF16))

        def accumulate(c, alpha, pv):
            acc_ref[c] = alpha * acc_ref[c] + pv

        ahead = [scores(*st) for st in steps[:SCORE_LOOKAHEAD]]
        pending = None
        for n, (start, _, c) in enumerate(steps):
            s = ahead.pop(0)
            if n + SCORE_LOOKAHEAD < len(steps):
                ahead.append(scores(*steps[n + SCORE_LOOKAHEAD]))
            alpha, pv = softmax(start, c, s)
            if pending is not None:
                accumulate(*pending)
            pending = (c, alpha, pv)
        accumulate(*pending)

    n_far = jnp.maximum(i - 1, 0)

    def far_body(pair, carry):
        update([(2 * pair, None), (2 * pair + 1, None)])
        return carry

    lax.fori_loop(0, n_far // 2, far_body, 0)

    @pl.when(i == 0)
    def _():
        update([(i, 0)])

    @pl.when(i % 2 == 1)
    def _():
        update([(i - 1, 1), (i, 0)])

    @pl.when(jnp.logical_and(i >= 2, i % 2 == 0))
    def _():
        update([(i - 2, None), (i - 1, 1), (i, 0)])

    lq = lq_ref[...]
    lam = (jnp.exp(jnp.sum(lq[0:1] * lq[1:2], axis=1, keepdims=True))
           - jnp.exp(jnp.sum(lq[2:3] * lq[3:4], axis=1, keepdims=True)) + lambda_init)
    for h in range(ATT_HEADS):
        c = 2 * h
        ot = (acc_ref[c, 0:dv, :] / acc_ref[c, dv:dv + 1, :]
              - lam * (acc_ref[c + 1, 0:dv, :] / acc_ref[c + 1, dv:dv + 1, :]))
        o = ot.T
        ms = jnp.mean(o * o, axis=-1, keepdims=True)
        o = o * lax.rsqrt(ms + SUBLN_EPS) * g_ref[...] * (1.0 - lambda_init)
        z = z_ref[0, :, h * LANES:(h + 1) * LANES]
        o_ref[0, :, h * LANES:(h + 1) * LANES] = (o * (z * _sigmoid(z))).astype(o_ref.dtype)


def _attention(qkv, gates, bias_tiles, lam_qk_l, subln_g_l, lambda_init):
    b, s, _ = qkv.shape
    t = bias_tiles.shape[-1]
    kern = functools.partial(_attn_kernel, tile=t, seq=s, lambda_init=lambda_init)
    chains = 2 * ATT_HEADS
    return pl.pallas_call(
        kern,
        grid=(b, s // t),
        in_specs=[pl.BlockSpec((1, t, ATT_QK_COLS), lambda bi, i: (bi, i, 0)),
                  pl.BlockSpec((1, s, ATT_QK_COLS), lambda bi, i: (bi, 0, 1)),
                  pl.BlockSpec((1, s, ATT_WIDTH), lambda bi, i: (bi, 0, 2)),
                  pl.BlockSpec((1, t, ATT_WIDTH), lambda bi, i: (bi, i, 0)),
                  pl.BlockSpec((ATT_HEADS, 2, t, t), lambda bi, i: (0, 0, 0, 0)),
                  pl.BlockSpec((4, ATT_QK_DIM), lambda bi, i: (0, 0)),
                  pl.BlockSpec((1, ATT_V_DIM), lambda bi, i: (0, 0))],
        out_specs=pl.BlockSpec((1, t, ATT_WIDTH), lambda bi, i: (bi, i, 0)),
        out_shape=jax.ShapeDtypeStruct((b, s, ATT_WIDTH), BF16),
        scratch_shapes=[pltpu.VMEM((chains, t, LANES), BF16),
                        pltpu.VMEM((ATT_HEADS, ATT_V_DIM + SUM_ROWS, s), BF16),
                        pltpu.VMEM((chains, t), F32),
                        pltpu.VMEM((chains, ATT_V_DIM + SUM_ROWS, t), F32)],
        compiler_params=pltpu.CompilerParams(
            dimension_semantics=("arbitrary", "arbitrary"),
            vmem_limit_bytes=VMEM_LIMIT),
        name="diff_attention",
    )(qkv, qkv, qkv, gates, bias_tiles, lam_qk_l, subln_g_l.reshape(1, ATT_V_DIM))


def _segment_ones():
    idx = np.arange(RWKV_WIDTH) // RWKV_HEAD
    return jnp.asarray((idx[:, None] == idx[None, :]).astype(np.float32), dtype=BF16)


def _rwkv_prep_kernel(p_ref, halo_ref, mu_ref, w0_ref, wup_ref, a0_ref, aup_ref, kkw_ref, ka_ref,
                      rk_ref, seg_ref, r_o, k_o, v_o, w_o, kk_o, b_o, bonus_o):
    i = pl.program_id(1)
    p = p_ref[0]
    last_prev = jnp.where(i > 0, halo_ref[0][SUBLANES - 1:SUBLANES, :], 0.0)
    row = lax.broadcasted_iota(jnp.int32, p.shape, 0)
    prev = jnp.where(row == 0, last_prev, pltpu.roll(p, 1, axis=0))
    p = p + (prev - p) * mu_ref[...]
    w3 = 3 * RWKV_WIDTH
    r = p[:, 0:RWKV_WIDTH]
    k = p[:, RWKV_WIDTH:2 * RWKV_WIDTH]
    v = p[:, 2 * RWKV_WIDTH:w3]
    wd = p[:, w3:w3 + DECAY_LORA]
    ad = p[:, w3 + DECAY_LORA:w3 + DECAY_LORA + ICLR_LORA]
    seg = seg_ref[...]

    x = -(w0_ref[...] + _dot_f32(jnp.tanh(wd), wup_ref[...]))
    softplus = jnp.maximum(x, 0.0) + jnp.log(1.0 + jnp.exp(-jnp.abs(x)))
    w = -softplus - 0.5
    decay = jnp.exp(-jnp.exp(w))
    a = _sigmoid(a0_ref[...] + _dot_f32(ad, aup_ref[...]))
    kk = k * kkw_ref[...]
    norm = jnp.sqrt(_seg_sum(kk * kk, seg))
    kk = kk / jnp.maximum(norm, 1e-12)
    k = k * (1.0 + (a - 1.0) * ka_ref[...])

    r_o[0] = r
    k_o[0] = k
    v_o[0] = v
    w_o[0] = decay
    kk_o[0] = kk
    b_o[0] = kk * a
    bonus_o[0] = _seg_sum(r * k * rk_ref[...], seg) * v


def _rwkv_prep(rw, mu, w0, w_up, a0, a_up, k_k, k_a, r_k):
    b, s, _ = rw.shape
    tm = min(TOK_TILE, s)
    row = lambda a, n: a.reshape(1, n).astype(F32)
    full = lambda shape: pl.BlockSpec(shape, lambda bi, i: (0,) * len(shape))
    out_spec = pl.BlockSpec((1, tm, RWKV_WIDTH), lambda bi, i: (bi, i, 0))
    out_sds = jax.ShapeDtypeStruct((b, s, RWKV_WIDTH), F32)
    halo_blocks = tm // SUBLANES
    return pl.pallas_call(
        _rwkv_prep_kernel,
        grid=(b, s // tm),
        in_specs=[pl.BlockSpec((1, tm, RWKV_SHIFT_COLS), lambda bi, i: (bi, i, 0)),
                  pl.BlockSpec((1, SUBLANES, RWKV_SHIFT_COLS),
                               lambda bi, i: (bi, jnp.maximum(i * halo_blocks - 1, 0), 0)),
                  full((1, RWKV_SHIFT_COLS)), full((1, RWKV_WIDTH)), full((DECAY_LORA, RWKV_WIDTH)),
                  full((1, RWKV_WIDTH)), full((ICLR_LORA, RWKV_WIDTH)), full((1, RWKV_WIDTH)),
                  full((1, RWKV_WIDTH)), full((1, RWKV_WIDTH)), full((RWKV_WIDTH, RWKV_WIDTH))],
        out_specs=[out_spec] * 7,
        out_shape=[out_sds] * 7,
        compiler_params=pltpu.CompilerParams(dimension_semantics=("arbitrary", "arbitrary"),
                                             vmem_limit_bytes=VMEM_LIMIT),
        name="rwkv_prep",
    )(rw, rw, row(mu, RWKV_SHIFT_COLS), row(w0, RWKV_WIDTH), w_up.astype(F32), row(a0, RWKV_WIDTH),
      a_up.astype(F32), row(k_k, RWKV_WIDTH), row(k_a, RWKV_WIDTH), row(r_k, RWKV_WIDTH), _segment_ones())


def _tree_sum(xs):
    xs = list(xs)
    while len(xs) > 1:
        xs = [xs[a] + xs[a + 1] for a in range(0, len(xs) - 1, 2)] + ([xs[-1]] if len(xs) % 2 else [])
    return xs[0]


def _sublane_allsum(x):
    for shift in (4, 2, 1):
        x = x + pltpu.roll(x, shift, axis=0)
    return x


def _rwkv_seq_kernel(r_ref, w_ref, k_ref, kk_ref, b_ref, v_ref, y_ref, s_ref, *, steps):
    @pl.when(pl.program_id(0) == 0)
    def _():
        s_ref[...] = jnp.zeros(s_ref.shape, F32)

    kslices = [slice(SUBLANES * kg, SUBLANES * (kg + 1)) for kg in range(K_GROUPS)]

    for g in range(V_GROUPS // VG_BATCH):
        groups = [VG_BATCH * g + a for a in range(VG_BATCH)]

        def body(t, state):
            kkp = [kk_ref[t, ks, :] for ks in kslices]
            wp = [w_ref[t, ks, :] for ks in kslices]
            bp = [b_ref[t, ks, :] for ks in kslices]
            kp = [k_ref[t, ks, :] for ks in kslices]
            rp = [r_ref[t, ks, :] for ks in kslices]
            new_state = []
            for a, vg in enumerate(groups):
                st = state[a]
                sa = _sublane_allsum(_tree_sum(st[kg] * kkp[kg] for kg in range(K_GROUPS)))
                vb = jnp.broadcast_to(v_ref[t, vg:vg + 1, :], (SUBLANES, LANES))
                st = tuple((st[kg] * wp[kg] - sa * bp[kg]) + vb * kp[kg] for kg in range(K_GROUPS))
                y = _sublane_allsum(_tree_sum(st[kg] * rp[kg] for kg in range(K_GROUPS)))
                y_ref[t, vg:vg + 1, :] = y[0:1]
                new_state.append(st)
            return tuple(new_state)

        init = tuple(tuple(s_ref[vg, ks, :] for ks in kslices) for vg in groups)
        final = lax.fori_loop(0, steps, body, init)
        for a, vg in enumerate(groups):
            for kg, ks in enumerate(kslices):
                s_ref[vg, ks, :] = final[a][kg]


def _rwkv_recurrence(r_c, w_c, k_c, kk_c, b_c, v_r):
    s = r_c.shape[0]
    tc = min(SEQ_CHUNK, s)
    col = pl.BlockSpec((tc, RWKV_HEAD, LANES), lambda i: (i, 0, 0))
    rowspec = pl.BlockSpec((tc, V_GROUPS, LANES), lambda i: (i, 0, 0))
    return pl.pallas_call(
        functools.partial(_rwkv_seq_kernel, steps=tc),
        grid=(s // tc,),
        in_specs=[col] * 5 + [rowspec],
        out_specs=rowspec,
        out_shape=jax.ShapeDtypeStruct((s, V_GROUPS, LANES), F32),
        scratch_shapes=[pltpu.VMEM((V_GROUPS, RWKV_HEAD, LANES), F32)],
        compiler_params=pltpu.CompilerParams(dimension_semantics=("arbitrary",),
                                             vmem_limit_bytes=VMEM_LIMIT),
        name="rwkv_recurrence",
    )(r_c, w_c, k_c, kk_c, b_c, v_r)


def _to_key_layout(x):
    b, s, _ = x.shape
    y = x.reshape(b, s, RWKV_HEADS, RWKV_HEAD).transpose(1, 3, 0, 2).reshape(s, RWKV_HEAD, b * RWKV_HEADS)
    return jnp.tile(y, (1, 1, V_PER_LANE))


def _to_value_layout(x):
    b, s, _ = x.shape
    y = x.reshape(b, s, RWKV_HEADS, V_GROUPS, V_PER_LANE).transpose(1, 3, 4, 0, 2)
    return y.reshape(s, V_GROUPS, V_PER_LANE * b * RWKV_HEADS)


def _from_value_layout(y, b):
    s = y.shape[0]
    x = y.reshape(s, V_GROUPS, V_PER_LANE, b, RWKV_HEADS).transpose(3, 0, 4, 1, 2)
    return x.reshape(b, s, RWKV_WIDTH)


def _shift_rows(u, halo_u, shift):
    rolled = pltpu.roll(u, shift, axis=0)
    head = jnp.where(lax.broadcasted_iota(jnp.int32, halo_u.shape, 0) < shift,
                     pltpu.roll(halo_u, shift, axis=0), rolled[0:SUBLANES])
    return jnp.concatenate([head, rolled[SUBLANES:]], axis=0)


def _out_kernel(x_ref, att_ref, gate_ref, halo_ref, y_ref, bonus_ref, cw_ref, lng_ref, lnb_ref, seg_ref,
                wout_ref, fg_ref, o_ref, *, final):
    i = pl.program_id(1)
    c0 = ATT_WIDTH
    cb = gate_ref[0, :, c0:c0 + CONV_WIDTH]
    cc = gate_ref[0, :, c0 + CONV_WIDTH:c0 + 2 * CONV_WIDTH]
    ch = gate_ref[0, :, c0 + 2 * CONV_WIDTH:c0 + 3 * CONV_WIDTH]
    zc = gate_ref[0, :, c0 + 3 * CONV_WIDTH:c0 + 4 * CONV_WIDTH]
    zr = gate_ref[0, :, c0 + 4 * CONV_WIDTH:c0 + 4 * CONV_WIDTH + RWKV_WIDTH]
    u = cc * ch
    halo_u = jnp.where(i > 0, halo_ref[0, :, c0 + CONV_WIDTH:c0 + 2 * CONV_WIDTH]
                       * halo_ref[0, :, c0 + 2 * CONV_WIDTH:c0 + 3 * CONV_WIDTH], 0.0)
    cw = cw_ref[...]
    conv = (cw[0:1] * _shift_rows(u, halo_u, 2) + cw[1:2] * _shift_rows(u, halo_u, 1)) + cw[2:3] * u
    cv = (cb * conv) * (zc * _sigmoid(zc))

    seg = seg_ref[...]
    y = y_ref[0]
    inv_n = 1.0 / RWKV_HEAD
    mean = _seg_sum(y, seg) * inv_n
    d = y - mean
    var = _seg_sum(d * d, seg) * inv_n
    yn = d * lax.rsqrt(var + GN_EPS) * lng_ref[...] + lnb_ref[...]
    rw = (yn + bonus_ref[0]) * (zr * _sigmoid(zr))

    upd = (_dot(att_ref[0], wout_ref[0:ATT_WIDTH, :])
           + _dot(cv.astype(BF16), wout_ref[ATT_WIDTH:ATT_WIDTH + CONV_WIDTH, :])
           + _dot(rw.astype(BF16), wout_ref[ATT_WIDTH + CONV_WIDTH:, :]))
    xn = x_ref[0] + upd
    if final:
        ms = jnp.mean(xn * xn, axis=-1, keepdims=True)
        xn = xn * lax.rsqrt(ms + NORM_EPS) * fg_ref[...]
    o_ref[0] = xn


def _out_proj(x, att, gates, y, bonus, conv_w_l, lnx_g_l, lnx_b_l, w_out_bf16, final_g, final):
    b, s, _ = x.shape
    tm = min(TOK_TILE, s)
    row = lambda a, n: a.reshape(1, n).astype(F32)
    full = lambda shape: pl.BlockSpec(shape, lambda bi, i: (0,) * len(shape))
    tok = lambda n: pl.BlockSpec((1, tm, n), lambda bi, i: (bi, i, 0))
    halo_blocks = tm // SUBLANES
    return pl.pallas_call(
        functools.partial(_out_kernel, final=final),
        grid=(b, s // tm),
        in_specs=[tok(D_MODEL), tok(ATT_WIDTH), tok(GATE_COLS),
                  pl.BlockSpec((1, SUBLANES, GATE_COLS),
                               lambda bi, i: (bi, jnp.maximum(i * halo_blocks - 1, 0), 0)),
                  tok(RWKV_WIDTH), tok(RWKV_WIDTH),
                  full((CONV_K, CONV_WIDTH)), full((1, RWKV_WIDTH)), full((1, RWKV_WIDTH)),
                  full((RWKV_WIDTH, RWKV_WIDTH)), full((D_MODEL, D_MODEL)), full((1, D_MODEL))],
        out_specs=tok(D_MODEL),
        out_shape=jax.ShapeDtypeStruct((b, s, D_MODEL), F32),
        compiler_params=pltpu.CompilerParams(dimension_semantics=("arbitrary", "arbitrary"),
                                             vmem_limit_bytes=VMEM_LIMIT),
        name="out_proj",
    )(x, att, gates, gates, y, bonus, conv_w_l.astype(F32),
      row(lnx_g_l, RWKV_WIDTH), row(lnx_b_l, RWKV_WIDTH), _segment_ones(), w_out_bf16,
      row(final_g, D_MODEL))


def kernel(x, norm_g, w_in, w_out, final_norm_g, rel_bias, lam_qk, subln_g, conv_w, rwkv_mu, w0, w_up,
           a0, a_up, k_k, k_a, r_k, lnx_g, lnx_b):
    b, s, _ = x.shape
    assert b * RWKV_HEADS * V_PER_LANE == LANES, "recurrence packs batch * heads * 4 values on the lanes"
    tile = min(ATT_TILE, s)
    bias = _bias_tiles(rel_bias, tile)
    w_in_b = w_in.astype(BF16)
    w_out_b = w_out.astype(BF16)
    x = x.astype(F32)
    for l in range(DEPTH):
        lambda_init = 0.8 - 0.6 * math.exp(-0.3 * l)
        qkv, gates, rw = _in_proj(x.reshape(b * s, D_MODEL), norm_g[l].astype(F32), w_in_b[l])
        qkv = qkv.reshape(b, s, QKV_COLS)
        gates = gates.reshape(b, s, GATE_COLS)
        rw = rw.reshape(b, s, RWKV_SHIFT_COLS)
        att = _attention(qkv, gates, bias, lam_qk[l].astype(F32), subln_g[l].astype(F32), lambda_init)
        r, k, v, decay, kk, bb, bonus = _rwkv_prep(rw, rwkv_mu[l], w0[l], w_up[l], a0[l], a_up[l],
                                                   k_k[l], k_a[l], r_k[l])
        y = _rwkv_recurrence(_to_key_layout(r), _to_key_layout(decay), _to_key_layout(k),
                             _to_key_layout(kk), _to_key_layout(bb), _to_value_layout(v))
        y = _from_value_layout(y, b)
        x = _out_proj(x, att, gates, y, bonus, conv_w[l], lnx_g[l], lnx_b[l], w_out_b[l],
                      final_norm_g, final=(l == DEPTH - 1))
    return x
```

```python
import functools
import math

import numpy as np
import jax
import jax.numpy as jnp
from jax import lax
from jax.experimental import pallas as pl
from jax.experimental.pallas import tpu as pltpu

F32 = jnp.float32
BF16 = jnp.bfloat16

D_MODEL = 1024
DEPTH = 4
ATT_HEADS = 4
ATT_QK_DIM = 64
ATT_V_DIM = 128
ATT_WIDTH = ATT_HEADS * ATT_V_DIM
ATT_QK_COLS = ATT_HEADS * 2 * ATT_QK_DIM
CONV_WIDTH = 256
CONV_K = 3
RWKV_HEAD = 64
RWKV_WIDTH = 256
RWKV_HEADS = 4
DECAY_LORA = 64
ICLR_LORA = 64
RWKV_SHIFT_COLS = 3 * RWKV_WIDTH + DECAY_LORA + ICLR_LORA
IN_COLS = 2 * ATT_QK_COLS + 2 * ATT_WIDTH + 4 * CONV_WIDTH + RWKV_SHIFT_COLS + RWKV_WIDTH
N_BUCKETS = 32
MAX_DISTANCE = 128
NEG_INF = -1e30
NORM_EPS = 1e-6
SUBLN_EPS = 1e-5
GN_EPS = 64e-5

QKV_COLS = 2 * ATT_QK_COLS + ATT_WIDTH
GATE_COLS = ATT_WIDTH + 4 * CONV_WIDTH + RWKV_WIDTH
_OFF_ZATT = QKV_COLS
_OFF_RW = _OFF_ZATT + ATT_WIDTH + 4 * CONV_WIDTH
_OFF_ZRW = _OFF_RW + RWKV_SHIFT_COLS

LOG2E = math.log2(math.e)
Q_SCALE = ATT_QK_DIM ** -0.5 * LOG2E

LANES = 128
SUBLANES = 8
VMEM_LIMIT = 56 * 1024 * 1024

ATT_TILE = 256
SCORE_LOOKAHEAD = 3
SUM_ROWS = 16
TOK_TILE = 512


def _dot(a, b):
    return jnp.dot(a, b, preferred_element_type=F32)


def _dot_nt(a, b):
    return lax.dot_general(a, b, (((1,), (1,)), ((), ())), preferred_element_type=F32)


def _split2(x):
    hi = x.astype(BF16)
    lo = (x - hi.astype(F32)).astype(BF16)
    return hi, lo


def _dot_f32(a, b):
    ah, al = _split2(a)
    bh, bl = _split2(b)
    return _dot(ah, bh) + (_dot(ah, bl) + _dot(al, bh))


def _seg_sum(x, seg):
    hi = x.astype(BF16)
    r1 = x - hi.astype(F32)
    mid = r1.astype(BF16)
    lo = (r1 - mid.astype(F32)).astype(BF16)
    return _dot(hi, seg) + (_dot(mid, seg) + _dot(lo, seg))


def _sigmoid(x):
    return 1.0 / (1.0 + jnp.exp(-x))


def _bucket_tiles(tile):
    key = np.arange(tile)[:, None]
    query = np.arange(tile)[None, :]
    out = []
    for off in (0, tile):
        dist = query - key + off
        n = np.maximum(dist, 0)
        max_exact = N_BUCKETS // 2
        nf = np.maximum(n, 1).astype(np.float32)
        large = max_exact + (np.log(nf / np.float32(max_exact)) / np.float32(math.log(MAX_DISTANCE / max_exact))
                             * np.float32(N_BUCKETS - max_exact)).astype(np.int32)
        large = np.minimum(large, N_BUCKETS - 1)
        bucket = np.where(n < max_exact, n, large)
        out.append(np.where(dist >= 0, bucket, -1))
    return np.stack(out).astype(np.int32)


def _bias_kernel(tab_ref, bucket_ref, o_ref):
    h = pl.program_id(0)
    bk = bucket_ref[0]
    far = tab_ref[N_BUCKETS - 1, h]
    val = jnp.zeros(bk.shape, F32)
    for i in range(N_BUCKETS - 1):
        val = jnp.where(bk == i, (tab_ref[i, h] - far) * LOG2E, val)
    o_ref[0, 0] = jnp.where(bk < 0, NEG_INF, val)


def _bias_tiles(rel_bias, tile):
    buckets = jnp.asarray(_bucket_tiles(tile))
    return pl.pallas_call(
        _bias_kernel,
        grid=(ATT_HEADS, 2),
        in_specs=[pl.BlockSpec(memory_space=pltpu.SMEM),
                  pl.BlockSpec((1, tile, tile), lambda h, d: (d, 0, 0))],
        out_specs=pl.BlockSpec((1, 1, tile, tile), lambda h, d: (h, d, 0, 0)),
        out_shape=jax.ShapeDtypeStruct((ATT_HEADS, 2, tile, tile), F32),
        name="bias_tiles",
    )(rel_bias.astype(F32), buckets)


def _col_chunks(start, stop, width=512):
    c = start
    while c < stop:
        yield c, min(width, stop - c)
        c += width


def _in_proj_kernel(x_ref, g_ref, w_ref, qkv_ref, gate_ref, rw_ref):
    x = x_ref[...]
    ms = jnp.mean(x * x, axis=-1, keepdims=True)
    h = (x * lax.rsqrt(ms + NORM_EPS) * g_ref[...]).astype(BF16)
    for c, n in _col_chunks(0, QKV_COLS, ATT_QK_COLS):
        acc = _dot(h, w_ref[:, c:c + n])
        if c < ATT_QK_COLS:
            acc = acc * Q_SCALE
        qkv_ref[:, c:c + n] = acc.astype(BF16)
    for c, n in _col_chunks(_OFF_ZATT, _OFF_RW):
        gate_ref[:, c - _OFF_ZATT:c - _OFF_ZATT + n] = _dot(h, w_ref[:, c:c + n])
    for c, n in _col_chunks(_OFF_RW, _OFF_ZRW):
        rw_ref[:, c - _OFF_RW:c - _OFF_RW + n] = _dot(h, w_ref[:, c:c + n])
    g0 = _OFF_RW - _OFF_ZATT
    gate_ref[:, g0:g0 + RWKV_WIDTH] = _dot(h, w_ref[:, _OFF_ZRW:IN_COLS])


def _in_proj(x2d, g, w_bf16):
    m = x2d.shape[0]
    tm = TOK_TILE
    return pl.pallas_call(
        _in_proj_kernel,
        grid=(m // tm,),
        in_specs=[pl.BlockSpec((tm, D_MODEL), lambda i: (i, 0)),
                  pl.BlockSpec((1, D_MODEL), lambda i: (0, 0)),
                  pl.BlockSpec((D_MODEL, IN_COLS), lambda i: (0, 0))],
        out_specs=[pl.BlockSpec((tm, QKV_COLS), lambda i: (i, 0)),
                   pl.BlockSpec((tm, GATE_COLS), lambda i: (i, 0)),
                   pl.BlockSpec((tm, RWKV_SHIFT_COLS), lambda i: (i, 0))],
        out_shape=[jax.ShapeDtypeStruct((m, QKV_COLS), BF16),
                   jax.ShapeDtypeStruct((m, GATE_COLS), F32),
                   jax.ShapeDtypeStruct((m, RWKV_SHIFT_COLS), F32)],
        compiler_params=pltpu.CompilerParams(dimension_semantics=("arbitrary",),
                                             vmem_limit_bytes=VMEM_LIMIT),
        name="in_proj",
    )(x2d, g.reshape(1, D_MODEL), w_bf16)


def _attn_kernel(q_ref, k_ref, v_ref, z_ref, bias_ref, lq_ref, g_ref, o_ref,
                 qz_ref, vt_ref, m_ref, acc_ref, *, tile, seq, lambda_init):
    i = pl.program_id(1)
    chains = 2 * ATT_HEADS
    dv = ATT_V_DIM

    @pl.when(i == 0)
    def _():
        def tr(c, carry):
            start = pl.multiple_of(c * tile, tile)
            for h in range(ATT_HEADS):
                blk = v_ref[0, pl.ds(start, tile), h * LANES:(h + 1) * LANES]
                vt_ref[h, 0:dv, pl.ds(start, tile)] = blk.astype(F32).T.astype(BF16)
                vt_ref[h, dv:dv + SUM_ROWS, pl.ds(start, tile)] = jnp.ones((SUM_ROWS, tile), BF16)
            return carry
        lax.fori_loop(0, seq // tile, tr, 0)

    q = q_ref[0]
    lane = lax.broadcasted_iota(jnp.int32, (tile, LANES), 1)
    for h in range(ATT_HEADS):
        qh = q[:, h * LANES:(h + 1) * LANES]
        qz_ref[2 * h] = jnp.where(lane < ATT_QK_DIM, qh, jnp.zeros_like(qh))
        qz_ref[2 * h + 1] = jnp.where(lane >= ATT_QK_DIM, qh, jnp.zeros_like(qh))

    m_ref[...] = jnp.full(m_ref.shape, NEG_INF, F32)
    acc_ref[...] = jnp.zeros(acc_ref.shape, F32)

    def update(jobs):
        steps = [(pl.multiple_of(j * tile, tile), bias_idx, c) for j, bias_idx in jobs for c in range(chains)]

        def scores(start, bias_idx, c):
            h = c // 2
            kb = k_ref[0, pl.ds(start, tile), h * LANES:(h + 1) * LANES]
            s = _dot_nt(kb, qz_ref[c])
            if bias_idx is not None:
                s = s + bias_ref[h, bias_idx]
            return s

        def softmax(start, c, s):
            m_prev = m_ref[c:c + 1, :]
            m_next = jnp.maximum(m_prev, jnp.max(s, axis=0, keepdims=True))
            alpha = jnp.exp2(m_prev - m_next)
            p = jnp.exp2(s - m_next)
            m_ref[c:c + 1, :] = m_next
            vtb = vt_ref[c // 2, :, pl.ds(start, tile)]
            return alpha, _dot(vtb, p.astype(BF16))

        def accumulate(c, alpha, pv):
            acc_ref[c] = alpha * acc_ref[c] + pv

        ahead = [scores(*st) for st in steps[:SCORE_LOOKAHEAD]]
        pending = None
        for n, (start, _, c) in enumerate(steps):
            s = ahead.pop(0)
            if n + SCORE_LOOKAHEAD < len(steps):
                ahead.append(scores(*steps[n + SCORE_LOOKAHEAD]))
            alpha, pv = softmax(start, c, s)
            if pending is not None:
                accumulate(*pending)
            pending = (c, alpha, pv)
        accumulate(*pending)

    n_far = jnp.maximum(i - 1, 0)

    def far_body(pair, carry):
        update([(2 * pair, None), (2 * pair + 1, None)])
        return carry

    lax.fori_loop(0, n_far // 2, far_body, 0)

    @pl.when(i == 0)
    def _():
        update([(i, 0)])

    @pl.when(i % 2 == 1)
    def _():
        update([(i - 1, 1), (i, 0)])

    @pl.when(jnp.logical_and(i >= 2, i % 2 == 0))
    def _():
        update([(i - 2, None), (i - 1, 1), (i, 0)])

    lq = lq_ref[...]
    lam = (jnp.exp(jnp.sum(lq[0:1] * lq[1:2], axis=1, keepdims=True))
           - jnp.exp(jnp.sum(lq[2:3] * lq[3:4], axis=1, keepdims=True)) + lambda_init)
    for h in range(ATT_HEADS):
        c = 2 * h
        ot = (acc_ref[c, 0:dv, :] / acc_ref[c, dv:dv + 1, :]
              - lam * (acc_ref[c + 1, 0:dv, :] / acc_ref[c + 1, dv:dv + 1, :]))
        o = ot.T
        ms = jnp.mean(o * o, axis=-1, keepdims=True)
        o = o * lax.rsqrt(ms + SUBLN_EPS) * g_ref[...] * (1.0 - lambda_init)
        z = z_ref[0, :, h * LANES:(h + 1) * LANES]
        o_ref[0, :, h * LANES:(h + 1) * LANES] = (o * (z * _sigmoid(z))).astype(o_ref.dtype)


def _attention(qkv, gates, bias_tiles, lam_qk_l, subln_g_l, lambda_init):
    b, s, _ = qkv.shape
    t = bias_tiles.shape[-1]
    kern = functools.partial(_attn_kernel, tile=t, seq=s, lambda_init=lambda_init)
    chains = 2 * ATT_HEADS
    return pl.pallas_call(
        kern,
        grid=(b, s // t),
        in_specs=[pl.BlockSpec((1, t, ATT_QK_COLS), lambda bi, i: (bi, i, 0)),
                  pl.BlockSpec((1, s, ATT_QK_COLS), lambda bi, i: (bi, 0, 1)),
                  pl.BlockSpec((1, s, ATT_WIDTH), lambda bi, i: (bi, 0, 2)),
                  pl.BlockSpec((1, t, ATT_WIDTH), lambda bi, i: (bi, i, 0)),
                  pl.BlockSpec((ATT_HEADS, 2, t, t), lambda bi, i: (0, 0, 0, 0)),
                  pl.BlockSpec((4, ATT_QK_DIM), lambda bi, i: (0, 0)),
                  pl.BlockSpec((1, ATT_V_DIM), lambda bi, i: (0, 0))],
        out_specs=pl.BlockSpec((1, t, ATT_WIDTH), lambda bi, i: (bi, i, 0)),
        out_shape=jax.ShapeDtypeStruct((b, s, ATT_WIDTH), BF16),
        scratch_shapes=[pltpu.VMEM((chains, t, LANES), BF16),
                        pltpu.VMEM((ATT_HEADS, ATT_V_DIM + SUM_ROWS, s), BF16),
                        pltpu.VMEM((chains, t), F32),
                        pltpu.VMEM((chains, ATT_V_DIM + SUM_ROWS, t), F32)],
        compiler_params=pltpu.CompilerParams(
            dimension_semantics=("arbitrary", "arbitrary"),
            vmem_limit_bytes=VMEM_LIMIT),
        name="diff_attention",
    )(qkv, qkv, qkv, gates, bias_tiles, lam_qk_l, subln_g_l.reshape(1, ATT_V_DIM))


def _segment_ones():
    idx = np.arange(RWKV_WIDTH) // RWKV_HEAD
    return jnp.asarray((idx[:, None] == idx[None, :]).astype(np.float32), dtype=BF16)


CHUNK = 64
CHUNK_GROUP = 4


def _dot_tn(a, b):
    return lax.dot_general(a, b, (((0,), (0,)), ((), ())), preferred_element_type=F32)


def _rwkv_kernel(p_ref, halo_ref, mu_ref, w0_ref, wup_ref, a0_ref, aup_ref, kkw_ref, ka_ref,
                 rk_ref, seg_ref, y_o, bonus_o, r_s, k_s, v_s, lw_s, a_s, b_s, state_ref):
    i = pl.program_id(1)

    @pl.when(i == 0)
    def _():
        state_ref[...] = jnp.zeros(state_ref.shape, F32)

    p = p_ref[0]
    tm = p.shape[0]
    last_prev = jnp.where(i > 0, halo_ref[0][SUBLANES - 1:SUBLANES, :], 0.0)
    row = lax.broadcasted_iota(jnp.int32, p.shape, 0)
    prev = jnp.where(row == 0, last_prev, pltpu.roll(p, 1, axis=0))
    p = p + (prev - p) * mu_ref[...]
    w3 = 3 * RWKV_WIDTH
    r = p[:, 0:RWKV_WIDTH]
    k = p[:, RWKV_WIDTH:2 * RWKV_WIDTH]
    v = p[:, 2 * RWKV_WIDTH:w3]
    wd = p[:, w3:w3 + DECAY_LORA]
    ad = p[:, w3 + DECAY_LORA:w3 + DECAY_LORA + ICLR_LORA]
    seg = seg_ref[...]

    x = -(w0_ref[...] + _dot_f32(jnp.tanh(wd), wup_ref[...]))
    softplus = jnp.maximum(x, 0.0) + jnp.log(1.0 + jnp.exp(-jnp.abs(x)))
    w = -softplus - 0.5
    a = _sigmoid(a0_ref[...] + _dot_f32(ad, aup_ref[...]))
    kk = k * kkw_ref[...]
    norm = jnp.sqrt(_seg_sum(kk * kk, seg))
    kk = kk / jnp.maximum(norm, 1e-12)
    k = k * (1.0 + (a - 1.0) * ka_ref[...])
    bonus_o[0] = _seg_sum(r * k * rk_ref[...], seg) * v

    r_s[...] = r
    k_s[...] = k
    v_s[...] = v
    lw_s[...] = -jnp.exp(w)
    a_s[...] = -kk
    b_s[...] = kk * a

    c = CHUNK
    heads = range(RWKV_HEADS)
    lane_w = lax.broadcasted_iota(jnp.int32, (1, RWKV_WIDTH), 1)
    head_mask = [lane_w // RWKV_HEAD == h for h in heads]
    rr = lax.broadcasted_iota(jnp.int32, (2 * c, 2 * c), 0)
    cc = lax.broadcasted_iota(jnp.int32, (2 * c, 2 * c), 1)
    causal = (cc % c) < jnp.where(rr < c, rr, rr - c + 1)
    top_right = lax.broadcasted_iota(jnp.int32, (c, 2 * c), 1) >= c
    r64 = lax.broadcasted_iota(jnp.int32, (c, c), 0)
    c64 = lax.broadcasted_iota(jnp.int32, (c, c), 1)
    tril_ones = (c64 <= r64).astype(BF16)
    eye = (r64 == c64).astype(F32)
    rw_ = lax.broadcasted_iota(jnp.int32, (RWKV_WIDTH, RWKV_WIDTH), 0)
    cw_ = lax.broadcasted_iota(jnp.int32, (RWKV_WIDTH, RWKV_WIDTH), 1)
    block_diag = (rw_ // RWKV_HEAD) == (cw_ // RWKV_HEAD)
    diag = rw_ == cw_
    zero = jnp.zeros((c, RWKV_WIDTH), F32)
    bf = lambda t: t.astype(BF16)

    group = min(CHUNK_GROUP, tm // c)
    pick = lambda parts: functools.reduce(
        lambda acc, hp: jnp.where(head_mask[hp[0]], hp[1], acc), enumerate(parts), zero)

    def chunks(n, carry):
        gs = range(group)
        streams = [(g, h) for g in gs for h in heads]
        rows = [pl.ds(pl.multiple_of((n * group + g) * c, c), c) for g in gs]
        rc, kc, vc, lw, ac, bc = ([ref[rows[g], :] for g in gs] for ref in (r_s, k_s, v_s, lw_s, a_s, b_s))
        cum = [_seg_sum_left(tril_ones, lw[g]) for g in gs]
        gam = [jnp.exp(cum[g]) for g in gs]
        gam_inv = [jnp.exp(-cum[g]) for g in gs]
        at = [ac[g] * jnp.exp(cum[g] - lw[g]) for g in gs]
        rt = [rc[g] * gam[g] for g in gs]
        bt = [bc[g] * gam_inv[g] for g in gs]
        kt = [kc[g] * gam_inv[g] for g in gs]
        g_end = [gam[g][c - 1:c, :] for g in gs]
        left = [jnp.concatenate([at[g], rt[g]], axis=0) for g in gs]
        right = [bf(jnp.concatenate([bt[g], kt[g]], axis=0)) for g in gs]

        gm = {(g, h): jnp.where(causal, _dot_nt(bf(jnp.where(head_mask[h], left[g], 0.0)), right[g]), 0.0)
              for g, h in streams}
        npow = {s: gm[s][0:c, 0:c] for s in streams}
        t_inv = {s: eye + npow[s] for s in streams}
        for _ in range(5):
            npow = {s: _dot(bf(npow[s]), bf(npow[s])) for s in streams}
            t_inv = {s: t_inv[s] + _dot(bf(t_inv[s]), bf(npow[s])) for s in streams}

        zero_v = [bf(jnp.concatenate([zero, vc[g]], axis=0)) for g in gs]
        q1_pre = {(g, h): _dot(bf(jnp.where(top_right, gm[g, h][0:c, :], 0.0)), zero_v[g])
                  for g, h in streams}
        sol = {(g, h): _dot(bf(t_inv[g, h]), bf(jnp.concatenate([at[g], q1_pre[g, h]], axis=1)))
               for g, h in streams}
        low = {(g, h): _dot(bf(gm[g, h][c:2 * c, :]),
                            bf(jnp.concatenate([sol[g, h], jnp.concatenate([zero, vc[g]], axis=1)], axis=0)))
               for g, h in streams}

        a_new = [pick([sol[g, h][:, :RWKV_WIDTH] for h in heads]) for g in gs]
        q1 = [pick([sol[g, h][:, RWKV_WIDTH:] for h in heads]) for g in gs]
        r_new = [rt[g] + pick([low[g, h][:, :RWKV_WIDTH] for h in heads]) for g in gs]
        q2 = [pick([low[g, h][:, RWKV_WIDTH:] for h in heads]) for g in gs]
        m_bd = [jnp.where(block_diag, _dot_tn(bf(bt[g] * g_end[g]), bf(a_new[g])), 0.0)
                + jnp.where(diag, g_end[g], 0.0) for g in gs]
        j_bd = [jnp.where(block_diag,
                          _dot_tn(bf(jnp.concatenate([bt[g] * g_end[g], kt[g] * g_end[g]], axis=0)),
                                  bf(jnp.concatenate([q1[g], vc[g]], axis=0))), 0.0) for g in gs]

        state = state_ref[...]
        for g in gs:
            y_o[0, rows[g], :] = _dot_f32(r_new[g], state) + q2[g]
            state = _dot_f32(m_bd[g], state) + j_bd[g]
        state_ref[...] = state
        return carry

    lax.fori_loop(0, tm // (c * group), chunks, 0)


def _seg_sum_left(ones_bf16, x):
    hi = x.astype(BF16)
    r1 = x - hi.astype(F32)
    mid = r1.astype(BF16)
    lo = (r1 - mid.astype(F32)).astype(BF16)
    return _dot(ones_bf16, hi) + (_dot(ones_bf16, mid) + _dot(ones_bf16, lo))


def _rwkv_mix(rw, mu, w0, w_up, a0, a_up, k_k, k_a, r_k):
    b, s, _ = rw.shape
    tm = min(TOK_TILE, s)
    row = lambda a, n: a.reshape(1, n).astype(F32)
    full = lambda shape: pl.BlockSpec(shape, lambda bi, i: (0,) * len(shape))
    out_spec = pl.BlockSpec((1, tm, RWKV_WIDTH), lambda bi, i: (bi, i, 0))
    out_sds = jax.ShapeDtypeStruct((b, s, RWKV_WIDTH), F32)
    halo_blocks = tm // SUBLANES
    tile_scratch = pltpu.VMEM((tm, RWKV_WIDTH), F32)
    return pl.pallas_call(
        _rwkv_kernel,
        grid=(b, s // tm),
        in_specs=[pl.BlockSpec((1, tm, RWKV_SHIFT_COLS), lambda bi, i: (bi, i, 0)),
                  pl.BlockSpec((1, SUBLANES, RWKV_SHIFT_COLS),
                               lambda bi, i: (bi, jnp.maximum(i * halo_blocks - 1, 0), 0)),
                  full((1, RWKV_SHIFT_COLS)), full((1, RWKV_WIDTH)), full((DECAY_LORA, RWKV_WIDTH)),
                  full((1, RWKV_WIDTH)), full((ICLR_LORA, RWKV_WIDTH)), full((1, RWKV_WIDTH)),
                  full((1, RWKV_WIDTH)), full((1, RWKV_WIDTH)), full((RWKV_WIDTH, RWKV_WIDTH))],
        out_specs=[out_spec] * 2,
        out_shape=[out_sds] * 2,
        scratch_shapes=[tile_scratch] * 6 + [pltpu.VMEM((RWKV_WIDTH, RWKV_WIDTH), F32)],
        compiler_params=pltpu.CompilerParams(dimension_semantics=("arbitrary", "arbitrary"),
                                             vmem_limit_bytes=VMEM_LIMIT),
        name="rwkv_mix",
    )(rw, rw, row(mu, RWKV_SHIFT_COLS), row(w0, RWKV_WIDTH), w_up.astype(F32), row(a0, RWKV_WIDTH),
      a_up.astype(F32), row(k_k, RWKV_WIDTH), row(k_a, RWKV_WIDTH), row(r_k, RWKV_WIDTH), _segment_ones())


def _shift_rows(u, halo_u, shift):
    rolled = pltpu.roll(u, shift, axis=0)
    head = jnp.where(lax.broadcasted_iota(jnp.int32, halo_u.shape, 0) < shift,
                     pltpu.roll(halo_u, shift, axis=0), rolled[0:SUBLANES])
    return jnp.concatenate([head, rolled[SUBLANES:]], axis=0)


def _out_kernel(x_ref, att_ref, gate_ref, halo_ref, y_ref, bonus_ref, cw_ref, lng_ref, lnb_ref, seg_ref,
                wout_ref, fg_ref, o_ref, *, final):
    i = pl.program_id(1)
    c0 = ATT_WIDTH
    cb = gate_ref[0, :, c0:c0 + CONV_WIDTH]
    cc = gate_ref[0, :, c0 + CONV_WIDTH:c0 + 2 * CONV_WIDTH]
    ch = gate_ref[0, :, c0 + 2 * CONV_WIDTH:c0 + 3 * CONV_WIDTH]
    zc = gate_ref[0, :, c0 + 3 * CONV_WIDTH:c0 + 4 * CONV_WIDTH]
    zr = gate_ref[0, :, c0 + 4 * CONV_WIDTH:c0 + 4 * CONV_WIDTH + RWKV_WIDTH]
    u = cc * ch
    halo_u = jnp.where(i > 0, halo_ref[0, :, c0 + CONV_WIDTH:c0 + 2 * CONV_WIDTH]
                       * halo_ref[0, :, c0 + 2 * CONV_WIDTH:c0 + 3 * CONV_WIDTH], 0.0)
    cw = cw_ref[...]
    conv = (cw[0:1] * _shift_rows(u, halo_u, 2) + cw[1:2] * _shift_rows(u, halo_u, 1)) + cw[2:3] * u
    cv = (cb * conv) * (zc * _sigmoid(zc))

    seg = seg_ref[...]
    y = y_ref[0]
    inv_n = 1.0 / RWKV_HEAD
    mean = _seg_sum(y, seg) * inv_n
    d = y - mean
    var = _seg_sum(d * d, seg) * inv_n
    yn = d * lax.rsqrt(var + GN_EPS) * lng_ref[...] + lnb_ref[...]
    rw = (yn + bonus_ref[0]) * (zr * _sigmoid(zr))

    upd = (_dot(att_ref[0], wout_ref[0:ATT_WIDTH, :])
           + _dot(cv.astype(BF16), wout_ref[ATT_WIDTH:ATT_WIDTH + CONV_WIDTH, :])
           + _dot(rw.astype(BF16), wout_ref[ATT_WIDTH + CONV_WIDTH:, :]))
    xn = x_ref[0] + upd
    if final:
        ms = jnp.mean(xn * xn, axis=-1, keepdims=True)
        xn = xn * lax.rsqrt(ms + NORM_EPS) * fg_ref[...]
    o_ref[0] = xn


def _out_proj(x, att, gates, y, bonus, conv_w_l, lnx_g_l, lnx_b_l, w_out_bf16, final_g, final):
    b, s, _ = x.shape
    tm = min(TOK_TILE, s)
    row = lambda a, n: a.reshape(1, n).astype(F32)
    full = lambda shape: pl.BlockSpec(shape, lambda bi, i: (0,) * len(shape))
    tok = lambda n: pl.BlockSpec((1, tm, n), lambda bi, i: (bi, i, 0))
    halo_blocks = tm // SUBLANES
    return pl.pallas_call(
        functools.partial(_out_kernel, final=final),
        grid=(b, s // tm),
        in_specs=[tok(D_MODEL), tok(ATT_WIDTH), tok(GATE_COLS),
                  pl.BlockSpec((1, SUBLANES, GATE_COLS),
                               lambda bi, i: (bi, jnp.maximum(i * halo_blocks - 1, 0), 0)),
                  tok(RWKV_WIDTH), tok(RWKV_WIDTH),
                  full((CONV_K, CONV_WIDTH)), full((1, RWKV_WIDTH)), full((1, RWKV_WIDTH)),
                  full((RWKV_WIDTH, RWKV_WIDTH)), full((D_MODEL, D_MODEL)), full((1, D_MODEL))],
        out_specs=tok(D_MODEL),
        out_shape=jax.ShapeDtypeStruct((b, s, D_MODEL), F32),
        compiler_params=pltpu.CompilerParams(dimension_semantics=("arbitrary", "arbitrary"),
                                             vmem_limit_bytes=VMEM_LIMIT),
        name="out_proj",
    )(x, att, gates, gates, y, bonus, conv_w_l.astype(F32),
      row(lnx_g_l, RWKV_WIDTH), row(lnx_b_l, RWKV_WIDTH), _segment_ones(), w_out_bf16,
      row(final_g, D_MODEL))


def kernel(x, norm_g, w_in, w_out, final_norm_g, rel_bias, lam_qk, subln_g, conv_w, rwkv_mu, w0, w_up,
           a0, a_up, k_k, k_a, r_k, lnx_g, lnx_b):
    b, s, _ = x.shape
    tile = min(ATT_TILE, s)
    bias = _bias_tiles(rel_bias, tile)
    w_in_b = w_in.astype(BF16)
    w_out_b = w_out.astype(BF16)
    x = x.astype(F32)
    for l in range(DEPTH):
        lambda_init = 0.8 - 0.6 * math.exp(-0.3 * l)
        qkv, gates, rw = _in_proj(x.reshape(b * s, D_MODEL), norm_g[l].astype(F32), w_in_b[l])
        qkv = qkv.reshape(b, s, QKV_COLS)
        gates = gates.reshape(b, s, GATE_COLS)
        rw = rw.reshape(b, s, RWKV_SHIFT_COLS)
        att = _attention(qkv, gates, bias, lam_qk[l].astype(F32), subln_g[l].astype(F32), lambda_init)
        y, bonus = _rwkv_mix(rw, rwkv_mu[l], w0[l], w_up[l], a0[l], a_up[l], k_k[l], k_a[l], r_k[l])
        x = _out_proj(x, att, gates, y, bonus, conv_w[l], lnx_g[l], lnx_b[l], w_out_b[l],
                      final_norm_g, final=(l == DEPTH - 1))
    return x
```

```python
import functools
import math

import numpy as np
import jax
import jax.numpy as jnp
from jax import lax
from jax.experimental import pallas as pl
from jax.experimental.pallas import tpu as pltpu

F32 = jnp.float32
BF16 = jnp.bfloat16

D_MODEL = 1024
DEPTH = 4
ATT_HEADS = 4
ATT_QK_DIM = 64
ATT_V_DIM = 128
ATT_WIDTH = ATT_HEADS * ATT_V_DIM
ATT_QK_COLS = ATT_HEADS * 2 * ATT_QK_DIM
CONV_WIDTH = 256
CONV_K = 3
RWKV_HEAD = 64
RWKV_WIDTH = 256
RWKV_HEADS = 4
DECAY_LORA = 64
ICLR_LORA = 64
RWKV_SHIFT_COLS = 3 * RWKV_WIDTH + DECAY_LORA + ICLR_LORA
IN_COLS = 2 * ATT_QK_COLS + 2 * ATT_WIDTH + 4 * CONV_WIDTH + RWKV_SHIFT_COLS + RWKV_WIDTH
N_BUCKETS = 32
MAX_DISTANCE = 128
NEG_INF = -1e30
NORM_EPS = 1e-6
SUBLN_EPS = 1e-5
GN_EPS = 64e-5

QKV_COLS = 2 * ATT_QK_COLS + ATT_WIDTH
GATE_COLS = ATT_WIDTH + 4 * CONV_WIDTH + RWKV_WIDTH
_OFF_ZATT = QKV_COLS
_OFF_RW = _OFF_ZATT + ATT_WIDTH + 4 * CONV_WIDTH
_OFF_ZRW = _OFF_RW + RWKV_SHIFT_COLS

LOG2E = math.log2(math.e)
Q_SCALE = ATT_QK_DIM ** -0.5 * LOG2E

LANES = 128
SUBLANES = 8
VMEM_LIMIT = 56 * 1024 * 1024

ATT_TILE = 256
SCORE_LOOKAHEAD = 6
SUM_ROWS = 16
TOK_TILE = 512


def _dot(a, b):
    return jnp.dot(a, b, preferred_element_type=F32)


def _dot_nt(a, b):
    return lax.dot_general(a, b, (((1,), (1,)), ((), ())), preferred_element_type=F32)


def _split2(x):
    hi = x.astype(BF16)
    lo = (x - hi.astype(F32)).astype(BF16)
    return hi, lo


def _dot_f32(a, b):
    ah, al = _split2(a)
    bh, bl = _split2(b)
    return _dot(ah, bh) + (_dot(ah, bl) + _dot(al, bh))


def _seg_sum(x, seg):
    hi, lo = _split2(x)
    return _dot(hi, seg) + _dot(lo, seg)


def _sigmoid(x):
    return 1.0 / (1.0 + jnp.exp(-x))


def _bucket_tiles(tile):
    key = np.arange(tile)[:, None]
    query = np.arange(tile)[None, :]
    out = []
    for off in (0, tile):
        dist = query - key + off
        n = np.maximum(dist, 0)
        max_exact = N_BUCKETS // 2
        nf = np.maximum(n, 1).astype(np.float32)
        large = max_exact + (np.log(nf / np.float32(max_exact)) / np.float32(math.log(MAX_DISTANCE / max_exact))
                             * np.float32(N_BUCKETS - max_exact)).astype(np.int32)
        large = np.minimum(large, N_BUCKETS - 1)
        bucket = np.where(n < max_exact, n, large)
        out.append(np.where(dist >= 0, bucket, -1))
    return np.stack(out).astype(np.int32)


def _bias_kernel(tab_ref, bucket_ref, o_ref):
    h = pl.program_id(0)
    bk = bucket_ref[0]
    far = tab_ref[N_BUCKETS - 1, h]
    val = jnp.zeros(bk.shape, F32)
    for i in range(N_BUCKETS - 1):
        val = jnp.where(bk == i, (tab_ref[i, h] - far) * LOG2E, val)
    o_ref[0, 0] = jnp.where(bk < 0, NEG_INF, val)


def _bias_tiles(rel_bias, tile):
    buckets = jnp.asarray(_bucket_tiles(tile))
    return pl.pallas_call(
        _bias_kernel,
        grid=(ATT_HEADS, 2),
        in_specs=[pl.BlockSpec(memory_space=pltpu.SMEM),
                  pl.BlockSpec((1, tile, tile), lambda h, d: (d, 0, 0))],
        out_specs=pl.BlockSpec((1, 1, tile, tile), lambda h, d: (h, d, 0, 0)),
        out_shape=jax.ShapeDtypeStruct((ATT_HEADS, 2, tile, tile), F32),
        name="bias_tiles",
    )(rel_bias.astype(F32), buckets)


def _col_chunks(start, stop, width=512):
    c = start
    while c < stop:
        yield c, min(width, stop - c)
        c += width


def _in_proj_kernel(x_ref, g_ref, w_ref, qkv_ref, gate_ref, rw_ref):
    x = x_ref[...]
    ms = jnp.mean(x * x, axis=-1, keepdims=True)
    h = (x * lax.rsqrt(ms + NORM_EPS) * g_ref[...]).astype(BF16)
    for c, n in _col_chunks(0, QKV_COLS, ATT_QK_COLS):
        acc = _dot(h, w_ref[:, c:c + n])
        if c < ATT_QK_COLS:
            acc = acc * Q_SCALE
        qkv_ref[:, c:c + n] = acc.astype(BF16)
    for c, n in _col_chunks(_OFF_ZATT, _OFF_RW):
        gate_ref[:, c - _OFF_ZATT:c - _OFF_ZATT + n] = _dot(h, w_ref[:, c:c + n])
    for c, n in _col_chunks(_OFF_RW, _OFF_ZRW):
        rw_ref[:, c - _OFF_RW:c - _OFF_RW + n] = _dot(h, w_ref[:, c:c + n])
    g0 = _OFF_RW - _OFF_ZATT
    gate_ref[:, g0:g0 + RWKV_WIDTH] = _dot(h, w_ref[:, _OFF_ZRW:IN_COLS])


def _in_proj(x2d, g, w_bf16):
    m = x2d.shape[0]
    tm = TOK_TILE
    return pl.pallas_call(
        _in_proj_kernel,
        grid=(m // tm,),
        in_specs=[pl.BlockSpec((tm, D_MODEL), lambda i: (i, 0)),
                  pl.BlockSpec((1, D_MODEL), lambda i: (0, 0)),
                  pl.BlockSpec((D_MODEL, IN_COLS), lambda i: (0, 0))],
        out_specs=[pl.BlockSpec((tm, QKV_COLS), lambda i: (i, 0)),
                   pl.BlockSpec((tm, GATE_COLS), lambda i: (i, 0)),
                   pl.BlockSpec((tm, RWKV_SHIFT_COLS), lambda i: (i, 0))],
        out_shape=[jax.ShapeDtypeStruct((m, QKV_COLS), BF16),
                   jax.ShapeDtypeStruct((m, GATE_COLS), F32),
                   jax.ShapeDtypeStruct((m, RWKV_SHIFT_COLS), F32)],
        compiler_params=pltpu.CompilerParams(dimension_semantics=("arbitrary",),
                                             vmem_limit_bytes=VMEM_LIMIT),
        name="in_proj",
    )(x2d, g.reshape(1, D_MODEL), w_bf16)


def _attn_kernel(q_ref, k_ref, v_ref, z_ref, bias_ref, lq_ref, g_ref, o_ref,
                 qz_ref, vt_ref, m_ref, acc_ref, *, tile, seq, lambda_init):
    i = pl.program_id(1)
    chains = 2 * ATT_HEADS
    dv = ATT_V_DIM

    @pl.when(i == 0)
    def _():
        def tr(c, carry):
            start = pl.multiple_of(c * tile, tile)
            for h in range(ATT_HEADS):
                blk = v_ref[0, pl.ds(start, tile), h * LANES:(h + 1) * LANES]
                vt_ref[h, 0:dv, pl.ds(start, tile)] = blk.astype(F32).T.astype(BF16)
                vt_ref[h, dv:dv + SUM_ROWS, pl.ds(start, tile)] = jnp.ones((SUM_ROWS, tile), BF16)
            return carry
        lax.fori_loop(0, seq // tile, tr, 0)

    q = q_ref[0]
    lane = lax.broadcasted_iota(jnp.int32, (tile, LANES), 1)
    for h in range(ATT_HEADS):
        qh = q[:, h * LANES:(h + 1) * LANES]
        qz_ref[2 * h] = jnp.where(lane < ATT_QK_DIM, qh, jnp.zeros_like(qh))
        qz_ref[2 * h + 1] = jnp.where(lane >= ATT_QK_DIM, qh, jnp.zeros_like(qh))

    m_ref[...] = jnp.full(m_ref.shape, NEG_INF, F32)
    acc_ref[...] = jnp.zeros(acc_ref.shape, F32)

    def update(jobs):
        steps = [(pl.multiple_of(j * tile, tile), bias_idx, c) for j, bias_idx in jobs for c in range(chains)]

        def scores(start, bias_idx, c):
            h = c // 2
            kb = k_ref[0, pl.ds(start, tile), h * LANES:(h + 1) * LANES]
            s = _dot_nt(kb, qz_ref[c])
            if bias_idx is not None:
                s = s + bias_ref[h, bias_idx]
            return s

        def softmax(start, c, s):
            m_prev = m_ref[c:c + 1, :]
            m_next = jnp.maximum(m_prev, jnp.max(s, axis=0, keepdims=True))
            alpha = jnp.exp2(m_prev - m_next)
            p = jnp.exp2(s - m_next)
            m_ref[c:c + 1, :] = m_next
            vtb = vt_ref[c // 2, :, pl.ds(start, tile)]
            return alpha, _dot(vtb, p.astype(BF16))

        def accumulate(c, alpha, pv):
            acc_ref[c] = alpha * acc_ref[c] + pv

        ahead = [scores(*st) for st in steps[:SCORE_LOOKAHEAD]]
        pending = None
        for n, (start, _, c) in enumerate(steps):
            s = ahead.pop(0)
            if n + SCORE_LOOKAHEAD < len(steps):
                ahead.append(scores(*steps[n + SCORE_LOOKAHEAD]))
            alpha, pv = softmax(start, c, s)
            if pending is not None:
                accumulate(*pending)
            pending = (c, alpha, pv)
        accumulate(*pending)

    n_far = jnp.maximum(i - 1, 0)

    def far_body(pair, carry):
        update([(2 * pair, None), (2 * pair + 1, None)])
        return carry

    lax.fori_loop(0, n_far // 2, far_body, 0)

    @pl.when(i == 0)
    def _():
        update([(i, 0)])

    @pl.when(i % 2 == 1)
    def _():
        update([(i - 1, 1), (i, 0)])

    @pl.when(jnp.logical_and(i >= 2, i % 2 == 0))
    def _():
        update([(i - 2, None), (i - 1, 1), (i, 0)])

    lq = lq_ref[...]
    lam = (jnp.exp(jnp.sum(lq[0:1] * lq[1:2], axis=1, keepdims=True))
           - jnp.exp(jnp.sum(lq[2:3] * lq[3:4], axis=1, keepdims=True)) + lambda_init)
    for h in range(ATT_HEADS):
        c = 2 * h
        ot = (acc_ref[c, 0:dv, :] / acc_ref[c, dv:dv + 1, :]
              - lam * (acc_ref[c + 1, 0:dv, :] / acc_ref[c + 1, dv:dv + 1, :]))
        o = ot.T
        ms = jnp.mean(o * o, axis=-1, keepdims=True)
        o = o * lax.rsqrt(ms + SUBLN_EPS) * g_ref[...] * (1.0 - lambda_init)
        z = z_ref[0, :, h * LANES:(h + 1) * LANES]
        o_ref[0, :, h * LANES:(h + 1) * LANES] = (o * (z * _sigmoid(z))).astype(o_ref.dtype)


def _attention(qkv, gates, bias_tiles, lam_qk_l, subln_g_l, lambda_init):
    b, s, _ = qkv.shape
    t = bias_tiles.shape[-1]
    kern = functools.partial(_attn_kernel, tile=t, seq=s, lambda_init=lambda_init)
    chains = 2 * ATT_HEADS
    return pl.pallas_call(
        kern,
        grid=(b, s // t),
        in_specs=[pl.BlockSpec((1, t, ATT_QK_COLS), lambda bi, i: (bi, i, 0)),
                  pl.BlockSpec((1, s, ATT_QK_COLS), lambda bi, i: (bi, 0, 1)),
                  pl.BlockSpec((1, s, ATT_WIDTH), lambda bi, i: (bi, 0, 2)),
                  pl.BlockSpec((1, t, ATT_WIDTH), lambda bi, i: (bi, i, 0)),
                  pl.BlockSpec((ATT_HEADS, 2, t, t), lambda bi, i: (0, 0, 0, 0)),
                  pl.BlockSpec((4, ATT_QK_DIM), lambda bi, i: (0, 0)),
                  pl.BlockSpec((1, ATT_V_DIM), lambda bi, i: (0, 0))],
        out_specs=pl.BlockSpec((1, t, ATT_WIDTH), lambda bi, i: (bi, i, 0)),
        out_shape=jax.ShapeDtypeStruct((b, s, ATT_WIDTH), BF16),
        scratch_shapes=[pltpu.VMEM((chains, t, LANES), BF16),
                        pltpu.VMEM((ATT_HEADS, ATT_V_DIM + SUM_ROWS, s), BF16),
                        pltpu.VMEM((chains, t), F32),
                        pltpu.VMEM((chains, ATT_V_DIM + SUM_ROWS, t), F32)],
        compiler_params=pltpu.CompilerParams(
            dimension_semantics=("arbitrary", "arbitrary"),
            vmem_limit_bytes=VMEM_LIMIT),
        name="diff_attention",
    )(qkv, qkv, qkv, gates, bias_tiles, lam_qk_l, subln_g_l.reshape(1, ATT_V_DIM))


def _segment_ones():
    idx = np.arange(RWKV_WIDTH) // RWKV_HEAD
    return jnp.asarray((idx[:, None] == idx[None, :]).astype(np.float32), dtype=BF16)


CHUNK = 64
CHUNK_GROUP = 8


def _dot_tn(a, b):
    return lax.dot_general(a, b, (((0,), (0,)), ((), ())), preferred_element_type=F32)


def _rwkv_kernel(p_ref, halo_ref, mu_ref, w0_ref, wup_ref, a0_ref, aup_ref, kkw_ref, ka_ref,
                 rk_ref, seg_ref, y_o, bonus_o, r_s, k_s, v_s, lw_s, a_s, b_s, state_ref):
    i = pl.program_id(1)

    @pl.when(i == 0)
    def _():
        state_ref[...] = jnp.zeros(state_ref.shape, F32)

    p = p_ref[0]
    tm = p.shape[0]
    last_prev = jnp.where(i > 0, halo_ref[0][SUBLANES - 1:SUBLANES, :], 0.0)
    row = lax.broadcasted_iota(jnp.int32, p.shape, 0)
    prev = jnp.where(row == 0, last_prev, pltpu.roll(p, 1, axis=0))
    p = p + (prev - p) * mu_ref[...]
    w3 = 3 * RWKV_WIDTH
    r = p[:, 0:RWKV_WIDTH]
    k = p[:, RWKV_WIDTH:2 * RWKV_WIDTH]
    v = p[:, 2 * RWKV_WIDTH:w3]
    wd = p[:, w3:w3 + DECAY_LORA]
    ad = p[:, w3 + DECAY_LORA:w3 + DECAY_LORA + ICLR_LORA]
    seg = seg_ref[...]

    x = -(w0_ref[...] + _dot_f32(jnp.tanh(wd), wup_ref[...]))
    softplus = jnp.maximum(x, 0.0) + jnp.log(1.0 + jnp.exp(-jnp.abs(x)))
    w = -softplus - 0.5
    a = _sigmoid(a0_ref[...] + _dot_f32(ad, aup_ref[...]))
    kk = k * kkw_ref[...]
    norm = jnp.sqrt(_seg_sum(kk * kk, seg))
    kk = kk / jnp.maximum(norm, 1e-12)
    k = k * (1.0 + (a - 1.0) * ka_ref[...])
    bonus_o[0] = _seg_sum(r * k * rk_ref[...], seg) * v

    r_s[...] = r
    k_s[...] = k
    v_s[...] = v
    lw_s[...] = -jnp.exp(w)
    a_s[...] = -kk
    b_s[...] = kk * a

    c = CHUNK
    heads = range(RWKV_HEADS)
    lane_w = lax.broadcasted_iota(jnp.int32, (1, RWKV_WIDTH), 1)
    head_mask = [lane_w // RWKV_HEAD == h for h in heads]
    rr = lax.broadcasted_iota(jnp.int32, (2 * c, 2 * c), 0)
    cc = lax.broadcasted_iota(jnp.int32, (2 * c, 2 * c), 1)
    causal = (cc % c) < jnp.where(rr < c, rr, rr - c + 1)
    top_right = lax.broadcasted_iota(jnp.int32, (c, 2 * c), 1) >= c
    r64 = lax.broadcasted_iota(jnp.int32, (c, c), 0)
    c64 = lax.broadcasted_iota(jnp.int32, (c, c), 1)
    tril_ones = (c64 <= r64).astype(BF16)
    eye = (r64 == c64).astype(F32)
    rw_ = lax.broadcasted_iota(jnp.int32, (RWKV_WIDTH, RWKV_WIDTH), 0)
    cw_ = lax.broadcasted_iota(jnp.int32, (RWKV_WIDTH, RWKV_WIDTH), 1)
    block_diag = (rw_ // RWKV_HEAD) == (cw_ // RWKV_HEAD)
    diag = rw_ == cw_
    zero = jnp.zeros((c, RWKV_WIDTH), F32)
    bf = lambda t: t.astype(BF16)

    group = min(CHUNK_GROUP, tm // c)
    pick = lambda parts: functools.reduce(
        lambda acc, hp: jnp.where(head_mask[hp[0]], hp[1], acc), enumerate(parts), zero)

    def chunks(n, carry):
        gs = range(group)
        streams = [(g, h) for g in gs for h in heads]
        rows = [pl.ds(pl.multiple_of((n * group + g) * c, c), c) for g in gs]
        rc, kc, vc, lw, ac, bc = ([ref[rows[g], :] for g in gs] for ref in (r_s, k_s, v_s, lw_s, a_s, b_s))
        cum = [_seg_sum_left(tril_ones, lw[g]) for g in gs]
        gam = [jnp.exp(cum[g]) for g in gs]
        gam_inv = [jnp.exp(-cum[g]) for g in gs]
        at = [ac[g] * jnp.exp(cum[g] - lw[g]) for g in gs]
        rt = [rc[g] * gam[g] for g in gs]
        bt = [bc[g] * gam_inv[g] for g in gs]
        kt = [kc[g] * gam_inv[g] for g in gs]
        g_end = [gam[g][c - 1:c, :] for g in gs]
        left = [jnp.concatenate([at[g], rt[g]], axis=0) for g in gs]
        right = [bf(jnp.concatenate([bt[g], kt[g]], axis=0)) for g in gs]

        gm = {(g, h): jnp.where(causal, _dot_nt(bf(jnp.where(head_mask[h], left[g], 0.0)), right[g]), 0.0)
              for g, h in streams}
        npow = {s: gm[s][0:c, 0:c] for s in streams}
        t_inv = {s: eye + npow[s] for s in streams}
        for _ in range(5):
            npow = {s: _dot(bf(npow[s]), bf(npow[s])) for s in streams}
            t_inv = {s: t_inv[s] + _dot(bf(t_inv[s]), bf(npow[s])) for s in streams}

        zero_v = [bf(jnp.concatenate([zero, vc[g]], axis=0)) for g in gs]
        q1_pre = {(g, h): _dot(bf(jnp.where(top_right, gm[g, h][0:c, :], 0.0)), zero_v[g])
                  for g, h in streams}
        sol = {(g, h): _dot(bf(t_inv[g, h]), bf(jnp.concatenate([at[g], q1_pre[g, h]], axis=1)))
               for g, h in streams}
        low = {(g, h): _dot(bf(gm[g, h][c:2 * c, :]),
                            bf(jnp.concatenate([sol[g, h], jnp.concatenate([zero, vc[g]], axis=1)], axis=0)))
               for g, h in streams}

        a_new = [pick([sol[g, h][:, :RWKV_WIDTH] for h in heads]) for g in gs]
        q1 = [pick([sol[g, h][:, RWKV_WIDTH:] for h in heads]) for g in gs]
        r_new = [rt[g] + pick([low[g, h][:, :RWKV_WIDTH] for h in heads]) for g in gs]
        q2 = [pick([low[g, h][:, RWKV_WIDTH:] for h in heads]) for g in gs]
        m_bd = [jnp.where(block_diag, _dot_tn(bf(bt[g] * g_end[g]), bf(a_new[g])), 0.0)
                + jnp.where(diag, g_end[g], 0.0) for g in gs]
        j_bd = [jnp.where(block_diag,
                          _dot_tn(bf(jnp.concatenate([bt[g] * g_end[g], kt[g] * g_end[g]], axis=0)),
                                  bf(jnp.concatenate([q1[g], vc[g]], axis=0))), 0.0) for g in gs]

        state = state_ref[...]
        for g in gs:
            y_o[0, rows[g], :] = _dot_f32(r_new[g], state) + q2[g]
            state = _dot_f32(m_bd[g], state) + j_bd[g]
        state_ref[...] = state
        return carry

    lax.fori_loop(0, tm // (c * group), chunks, 0)


def _seg_sum_left(ones_bf16, x):
    hi, lo = _split2(x)
    return _dot(ones_bf16, hi) + _dot(ones_bf16, lo)


def _rwkv_mix(rw, mu, w0, w_up, a0, a_up, k_k, k_a, r_k):
    b, s, _ = rw.shape
    tm = min(TOK_TILE, s)
    row = lambda a, n: a.reshape(1, n).astype(F32)
    full = lambda shape: pl.BlockSpec(shape, lambda bi, i: (0,) * len(shape))
    out_spec = pl.BlockSpec((1, tm, RWKV_WIDTH), lambda bi, i: (bi, i, 0))
    out_sds = jax.ShapeDtypeStruct((b, s, RWKV_WIDTH), F32)
    halo_blocks = tm // SUBLANES
    tile_scratch = pltpu.VMEM((tm, RWKV_WIDTH), F32)
    return pl.pallas_call(
        _rwkv_kernel,
        grid=(b, s // tm),
        in_specs=[pl.BlockSpec((1, tm, RWKV_SHIFT_COLS), lambda bi, i: (bi, i, 0)),
                  pl.BlockSpec((1, SUBLANES, RWKV_SHIFT_COLS),
                               lambda bi, i: (bi, jnp.maximum(i * halo_blocks - 1, 0), 0)),
                  full((1, RWKV_SHIFT_COLS)), full((1, RWKV_WIDTH)), full((DECAY_LORA, RWKV_WIDTH)),
                  full((1, RWKV_WIDTH)), full((ICLR_LORA, RWKV_WIDTH)), full((1, RWKV_WIDTH)),
                  full((1, RWKV_WIDTH)), full((1, RWKV_WIDTH)), full((RWKV_WIDTH, RWKV_WIDTH))],
        out_specs=[out_spec] * 2,
        out_shape=[out_sds] * 2,
        scratch_shapes=[tile_scratch] * 6 + [pltpu.VMEM((RWKV_WIDTH, RWKV_WIDTH), F32)],
        compiler_params=pltpu.CompilerParams(dimension_semantics=("arbitrary", "arbitrary"),
                                             vmem_limit_bytes=VMEM_LIMIT),
        name="rwkv_mix",
    )(rw, rw, row(mu, RWKV_SHIFT_COLS), row(w0, RWKV_WIDTH), w_up.astype(F32), row(a0, RWKV_WIDTH),
      a_up.astype(F32), row(k_k, RWKV_WIDTH), row(k_a, RWKV_WIDTH), row(r_k, RWKV_WIDTH), _segment_ones())


def _shift_rows(u, halo_u, shift):
    rolled = pltpu.roll(u, shift, axis=0)
    head = jnp.where(lax.broadcasted_iota(jnp.int32, halo_u.shape, 0) < shift,
                     pltpu.roll(halo_u, shift, axis=0), rolled[0:SUBLANES])
    return jnp.concatenate([head, rolled[SUBLANES:]], axis=0)


def _out_kernel(x_ref, att_ref, gate_ref, halo_ref, y_ref, bonus_ref, cw_ref, lng_ref, lnb_ref, seg_ref,
                wout_ref, fg_ref, o_ref, *, final):
    i = pl.program_id(1)
    c0 = ATT_WIDTH
    cb = gate_ref[0, :, c0:c0 + CONV_WIDTH]
    cc = gate_ref[0, :, c0 + CONV_WIDTH:c0 + 2 * CONV_WIDTH]
    ch = gate_ref[0, :, c0 + 2 * CONV_WIDTH:c0 + 3 * CONV_WIDTH]
    zc = gate_ref[0, :, c0 + 3 * CONV_WIDTH:c0 + 4 * CONV_WIDTH]
    zr = gate_ref[0, :, c0 + 4 * CONV_WIDTH:c0 + 4 * CONV_WIDTH + RWKV_WIDTH]
    u = cc * ch
    halo_u = jnp.where(i > 0, halo_ref[0, :, c0 + CONV_WIDTH:c0 + 2 * CONV_WIDTH]
                       * halo_ref[0, :, c0 + 2 * CONV_WIDTH:c0 + 3 * CONV_WIDTH], 0.0)
    cw = cw_ref[...]
    conv = (cw[0:1] * _shift_rows(u, halo_u, 2) + cw[1:2] * _shift_rows(u, halo_u, 1)) + cw[2:3] * u
    cv = (cb * conv) * (zc * _sigmoid(zc))

    seg = seg_ref[...]
    y = y_ref[0]
    inv_n = 1.0 / RWKV_HEAD
    mean = _seg_sum(y, seg) * inv_n
    d = y - mean
    var = _seg_sum(d * d, seg) * inv_n
    yn = d * lax.rsqrt(var + GN_EPS) * lng_ref[...] + lnb_ref[...]
    rw = (yn + bonus_ref[0]) * (zr * _sigmoid(zr))

    upd = (_dot(att_ref[0], wout_ref[0:ATT_WIDTH, :])
           + _dot(cv.astype(BF16), wout_ref[ATT_WIDTH:ATT_WIDTH + CONV_WIDTH, :])
           + _dot(rw.astype(BF16), wout_ref[ATT_WIDTH + CONV_WIDTH:, :]))
    xn = x_ref[0] + upd
    if final:
        ms = jnp.mean(xn * xn, axis=-1, keepdims=True)
        xn = xn * lax.rsqrt(ms + NORM_EPS) * fg_ref[...]
    o_ref[0] = xn


def _out_proj(x, att, gates, y, bonus, conv_w_l, lnx_g_l, lnx_b_l, w_out_bf16, final_g, final):
    b, s, _ = x.shape
    tm = min(TOK_TILE, s)
    row = lambda a, n: a.reshape(1, n).astype(F32)
    full = lambda shape: pl.BlockSpec(shape, lambda bi, i: (0,) * len(shape))
    tok = lambda n: pl.BlockSpec((1, tm, n), lambda bi, i: (bi, i, 0))
    halo_blocks = tm // SUBLANES
    return pl.pallas_call(
        functools.partial(_out_kernel, final=final),
        grid=(b, s // tm),
        in_specs=[tok(D_MODEL), tok(ATT_WIDTH), tok(GATE_COLS),
                  pl.BlockSpec((1, SUBLANES, GATE_COLS),
                               lambda bi, i: (bi, jnp.maximum(i * halo_blocks - 1, 0), 0)),
                  tok(RWKV_WIDTH), tok(RWKV_WIDTH),
                  full((CONV_K, CONV_WIDTH)), full((1, RWKV_WIDTH)), full((1, RWKV_WIDTH)),
                  full((RWKV_WIDTH, RWKV_WIDTH)), full((D_MODEL, D_MODEL)), full((1, D_MODEL))],
        out_specs=tok(D_MODEL),
        out_shape=jax.ShapeDtypeStruct((b, s, D_MODEL), F32),
        compiler_params=pltpu.CompilerParams(dimension_semantics=("arbitrary", "arbitrary"),
                                             vmem_limit_bytes=VMEM_LIMIT),
        name="out_proj",
    )(x, att, gates, gates, y, bonus, conv_w_l.astype(F32),
      row(lnx_g_l, RWKV_WIDTH), row(lnx_b_l, RWKV_WIDTH), _segment_ones(), w_out_bf16,
      row(final_g, D_MODEL))


def kernel(x, norm_g, w_in, w_out, final_norm_g, rel_bias, lam_qk, subln_g, conv_w, rwkv_mu, w0, w_up,
           a0, a_up, k_k, k_a, r_k, lnx_g, lnx_b):
    b, s, _ = x.shape
    tile = min(ATT_TILE, s)
    bias = _bias_tiles(rel_bias, tile)
    w_in_b = w_in.astype(BF16)
    w_out_b = w_out.astype(BF16)
    x = x.astype(F32)
    for l in range(DEPTH):
        lambda_init = 0.8 - 0.6 * math.exp(-0.3 * l)
        qkv, gates, rw = _in_proj(x.reshape(b * s, D_MODEL), norm_g[l].astype(F32), w_in_b[l])
        qkv = qkv.reshape(b, s, QKV_COLS)
        gates = gates.reshape(b, s, GATE_COLS)
        rw = rw.reshape(b, s, RWKV_SHIFT_COLS)
        att = _attention(qkv, gates, bias, lam_qk[l].astype(F32), subln_g[l].astype(F32), lambda_init)
        y, bonus = _rwkv_mix(rw, rwkv_mu[l], w0[l], w_up[l], a0[l], a_up[l], k_k[l], k_a[l], r_k[l])
        x = _out_proj(x, att, gates, y, bonus, conv_w[l], lnx_g[l], lnx_b[l], w_out_b[l],
                      final_norm_g, final=(l == DEPTH - 1))
    return x
```

```python
import functools
import math

import numpy as np
import jax
import jax.numpy as jnp
from jax import lax
from jax.experimental import pallas as pl
from jax.experimental.pallas import tpu as pltpu

F32 = jnp.float32
BF16 = jnp.bfloat16

D_MODEL = 1024
DEPTH = 4
ATT_HEADS = 4
ATT_QK_DIM = 64
ATT_V_DIM = 128
ATT_WIDTH = ATT_HEADS * ATT_V_DIM
ATT_QK_COLS = ATT_HEADS * 2 * ATT_QK_DIM
CONV_WIDTH = 256
CONV_K = 3
RWKV_HEAD = 64
RWKV_WIDTH = 256
RWKV_HEADS = 4
DECAY_LORA = 64
ICLR_LORA = 64
RWKV_SHIFT_COLS = 3 * RWKV_WIDTH + DECAY_LORA + ICLR_LORA
IN_COLS = 2 * ATT_QK_COLS + 2 * ATT_WIDTH + 4 * CONV_WIDTH + RWKV_SHIFT_COLS + RWKV_WIDTH
N_BUCKETS = 32
MAX_DISTANCE = 128
NEG_INF = -1e30
NORM_EPS = 1e-6
SUBLN_EPS = 1e-5
GN_EPS = 64e-5

QKV_COLS = 2 * ATT_QK_COLS + ATT_WIDTH
GATE_COLS = ATT_WIDTH + 4 * CONV_WIDTH + RWKV_WIDTH
_OFF_ZATT = QKV_COLS
_OFF_RW = _OFF_ZATT + ATT_WIDTH + 4 * CONV_WIDTH
_OFF_ZRW = _OFF_RW + RWKV_SHIFT_COLS

LOG2E = math.log2(math.e)
Q_SCALE = ATT_QK_DIM ** -0.5 * LOG2E

LANES = 128
SUBLANES = 8
HALO_ROWS = 16
VMEM_LIMIT = 56 * 1024 * 1024

ATT_TILE = 256
SCORE_LOOKAHEAD = 6
FAR_GROUP = 4
SUM_ROWS = 16
TOK_TILE = 512


def _dot(a, b):
    return jnp.dot(a, b, preferred_element_type=F32)


def _dot_nt(a, b):
    return lax.dot_general(a, b, (((1,), (1,)), ((), ())), preferred_element_type=F32)


def _split2(x):
    hi = x.astype(BF16)
    lo = (x - hi.astype(F32)).astype(BF16)
    return hi, lo


def _dot_f32(a, b):
    ah, al = _split2(a)
    bh, bl = _split2(b)
    return _dot(ah, bh) + (_dot(ah, bl) + _dot(al, bh))


def _seg_sum(x, seg):
    hi, lo = _split2(x)
    return _dot(hi, seg) + _dot(lo, seg)


def _sigmoid(x):
    return 1.0 / (1.0 + jnp.exp(-x))


def _bucket_tiles(tile):
    key = np.arange(tile)[:, None]
    query = np.arange(tile)[None, :]
    out = []
    for off in (0, tile):
        dist = query - key + off
        n = np.maximum(dist, 0)
        max_exact = N_BUCKETS // 2
        nf = np.maximum(n, 1).astype(np.float32)
        large = max_exact + (np.log(nf / np.float32(max_exact)) / np.float32(math.log(MAX_DISTANCE / max_exact))
                             * np.float32(N_BUCKETS - max_exact)).astype(np.int32)
        large = np.minimum(large, N_BUCKETS - 1)
        bucket = np.where(n < max_exact, n, large)
        out.append(np.where(dist >= 0, bucket, -1))
    return np.stack(out).astype(np.int32)


def _bias_kernel(tab_ref, bucket_ref, o_ref):
    h = pl.program_id(0)
    bk = bucket_ref[0]
    far = tab_ref[N_BUCKETS - 1, h]
    val = jnp.zeros(bk.shape, F32)
    for i in range(N_BUCKETS - 1):
        val = jnp.where(bk == i, (tab_ref[i, h] - far) * LOG2E, val)
    o_ref[0, 0] = jnp.where(bk < 0, NEG_INF, val)


def _bias_tiles(rel_bias, tile):
    buckets = jnp.asarray(_bucket_tiles(tile))
    return pl.pallas_call(
        _bias_kernel,
        grid=(ATT_HEADS, 2),
        in_specs=[pl.BlockSpec(memory_space=pltpu.SMEM),
                  pl.BlockSpec((1, tile, tile), lambda h, d: (d, 0, 0))],
        out_specs=pl.BlockSpec((1, 1, tile, tile), lambda h, d: (h, d, 0, 0)),
        out_shape=jax.ShapeDtypeStruct((ATT_HEADS, 2, tile, tile), F32),
        name="bias_tiles",
    )(rel_bias.astype(F32), buckets)


def _col_chunks(start, stop, width=512):
    c = start
    while c < stop:
        yield c, min(width, stop - c)
        c += width


def _in_proj_kernel(x_ref, g_ref, w_ref, qkv_ref, gate_ref, rw_ref):
    x = x_ref[...]
    ms = jnp.mean(x * x, axis=-1, keepdims=True)
    h = (x * lax.rsqrt(ms + NORM_EPS) * g_ref[...]).astype(BF16)
    for c, n in _col_chunks(0, QKV_COLS, ATT_QK_COLS):
        acc = _dot(h, w_ref[:, c:c + n])
        if c < ATT_QK_COLS:
            acc = acc * Q_SCALE
        qkv_ref[:, c:c + n] = acc.astype(BF16)
    for c, n in _col_chunks(_OFF_ZATT, _OFF_RW):
        gate_ref[:, c - _OFF_ZATT:c - _OFF_ZATT + n] = _dot(h, w_ref[:, c:c + n]).astype(BF16)
    for c, n in _col_chunks(_OFF_RW, _OFF_ZRW):
        rw_ref[:, c - _OFF_RW:c - _OFF_RW + n] = _dot(h, w_ref[:, c:c + n])
    g0 = _OFF_RW - _OFF_ZATT
    gate_ref[:, g0:g0 + RWKV_WIDTH] = _dot(h, w_ref[:, _OFF_ZRW:IN_COLS]).astype(BF16)


def _in_proj(x2d, g, w_bf16):
    m = x2d.shape[0]
    tm = TOK_TILE
    return pl.pallas_call(
        _in_proj_kernel,
        grid=(m // tm,),
        in_specs=[pl.BlockSpec((tm, D_MODEL), lambda i: (i, 0)),
                  pl.BlockSpec((1, D_MODEL), lambda i: (0, 0)),
                  pl.BlockSpec((D_MODEL, IN_COLS), lambda i: (0, 0))],
        out_specs=[pl.BlockSpec((tm, QKV_COLS), lambda i: (i, 0)),
                   pl.BlockSpec((tm, GATE_COLS), lambda i: (i, 0)),
                   pl.BlockSpec((tm, RWKV_SHIFT_COLS), lambda i: (i, 0))],
        out_shape=[jax.ShapeDtypeStruct((m, QKV_COLS), BF16),
                   jax.ShapeDtypeStruct((m, GATE_COLS), BF16),
                   jax.ShapeDtypeStruct((m, RWKV_SHIFT_COLS), F32)],
        compiler_params=pltpu.CompilerParams(dimension_semantics=("arbitrary",),
                                             vmem_limit_bytes=VMEM_LIMIT),
        name="in_proj",
    )(x2d, g.reshape(1, D_MODEL), w_bf16)


def _attn_kernel(q_ref, k_ref, v_ref, z_ref, bias_ref, lq_ref, g_ref, o_ref,
                 qz_ref, vt_ref, m_ref, acc_ref, *, tile, seq, lambda_init):
    i = pl.program_id(1)
    chains = 2 * ATT_HEADS
    dv = ATT_V_DIM

    @pl.when(i == 0)
    def _():
        def tr(c, carry):
            start = pl.multiple_of(c * tile, tile)
            for h in range(ATT_HEADS):
                blk = v_ref[0, pl.ds(start, tile), h * LANES:(h + 1) * LANES]
                vt_ref[h, 0:dv, pl.ds(start, tile)] = blk.astype(F32).T.astype(BF16)
                vt_ref[h, dv:dv + SUM_ROWS, pl.ds(start, tile)] = jnp.ones((SUM_ROWS, tile), BF16)
            return carry
        lax.fori_loop(0, seq // tile, tr, 0)

    q = q_ref[0]
    lane = lax.broadcasted_iota(jnp.int32, (tile, LANES), 1)
    for h in range(ATT_HEADS):
        qh = q[:, h * LANES:(h + 1) * LANES]
        qz_ref[2 * h] = jnp.where(lane < ATT_QK_DIM, qh, jnp.zeros_like(qh))
        qz_ref[2 * h + 1] = jnp.where(lane >= ATT_QK_DIM, qh, jnp.zeros_like(qh))

    m_ref[...] = jnp.full(m_ref.shape, NEG_INF, F32)
    acc_ref[...] = jnp.zeros(acc_ref.shape, F32)

    def update(jobs):
        steps = [(pl.multiple_of(j * tile, tile), bias_idx, c) for j, bias_idx in jobs for c in range(chains)]

        def scores(start, bias_idx, c):
            h = c // 2
            kb = k_ref[0, pl.ds(start, tile), h * LANES:(h + 1) * LANES]
            s = _dot_nt(kb, qz_ref[c])
            if bias_idx is not None:
                s = s + bias_ref[h, bias_idx]
            return s

        def softmax(start, c, s):
            m_prev = m_ref[c:c + 1, :]
            m_next = jnp.maximum(m_prev, jnp.max(s, axis=0, keepdims=True))
            alpha = jnp.exp2(m_prev - m_next)
            p = jnp.exp2(s - m_next)
            m_ref[c:c + 1, :] = m_next
            vtb = vt_ref[c // 2, :, pl.ds(start, tile)]
            return alpha, _dot(vtb, p.astype(BF16))

        def accumulate(c, alpha, pv):
            acc_ref[c] = alpha * acc_ref[c] + pv

        ahead = [scores(*st) for st in steps[:SCORE_LOOKAHEAD]]
        pending = None
        for n, (start, _, c) in enumerate(steps):
            s = ahead.pop(0)
            if n + SCORE_LOOKAHEAD < len(steps):
                ahead.append(scores(*steps[n + SCORE_LOOKAHEAD]))
            alpha, pv = softmax(start, c, s)
            if pending is not None:
                accumulate(*pending)
            pending = (c, alpha, pv)
        accumulate(*pending)

    n_far = jnp.maximum(i - 1, 0)
    fg = FAR_GROUP

    def far_body(trip, carry):
        update([(fg * trip + g, None) for g in range(fg)])
        return carry

    lax.fori_loop(0, n_far // fg, far_body, 0)

    @pl.when(i == 0)
    def _():
        update([(i, 0)])

    for rem in range(fg):
        @pl.when(jnp.logical_and(i >= 1, n_far % fg == rem))
        def _():
            update([(i - 1 - rem + g, None) for g in range(rem)] + [(i - 1, 1), (i, 0)])

    lq = lq_ref[...]
    lam = (jnp.exp(jnp.sum(lq[0:1] * lq[1:2], axis=1, keepdims=True))
           - jnp.exp(jnp.sum(lq[2:3] * lq[3:4], axis=1, keepdims=True)) + lambda_init)
    for h in range(ATT_HEADS):
        c = 2 * h
        ot = (acc_ref[c, 0:dv, :] / acc_ref[c, dv:dv + 1, :]
              - lam * (acc_ref[c + 1, 0:dv, :] / acc_ref[c + 1, dv:dv + 1, :]))
        o = ot.T
        ms = jnp.mean(o * o, axis=-1, keepdims=True)
        o = o * lax.rsqrt(ms + SUBLN_EPS) * g_ref[...] * (1.0 - lambda_init)
        z = z_ref[0, :, h * LANES:(h + 1) * LANES].astype(F32)
        o_ref[0, :, h * LANES:(h + 1) * LANES] = (o * (z * _sigmoid(z))).astype(o_ref.dtype)


def _attention(qkv, gates, bias_tiles, lam_qk_l, subln_g_l, lambda_init):
    b, s, _ = qkv.shape
    t = bias_tiles.shape[-1]
    kern = functools.partial(_attn_kernel, tile=t, seq=s, lambda_init=lambda_init)
    chains = 2 * ATT_HEADS
    return pl.pallas_call(
        kern,
        grid=(b, s // t),
        in_specs=[pl.BlockSpec((1, t, ATT_QK_COLS), lambda bi, i: (bi, i, 0)),
                  pl.BlockSpec((1, s, ATT_QK_COLS), lambda bi, i: (bi, 0, 1)),
                  pl.BlockSpec((1, s, ATT_WIDTH), lambda bi, i: (bi, 0, 2)),
                  pl.BlockSpec((1, t, ATT_WIDTH), lambda bi, i: (bi, i, 0)),
                  pl.BlockSpec((ATT_HEADS, 2, t, t), lambda bi, i: (0, 0, 0, 0)),
                  pl.BlockSpec((4, ATT_QK_DIM), lambda bi, i: (0, 0)),
                  pl.BlockSpec((1, ATT_V_DIM), lambda bi, i: (0, 0))],
        out_specs=pl.BlockSpec((1, t, ATT_WIDTH), lambda bi, i: (bi, i, 0)),
        out_shape=jax.ShapeDtypeStruct((b, s, ATT_WIDTH), BF16),
        scratch_shapes=[pltpu.VMEM((chains, t, LANES), BF16),
                        pltpu.VMEM((ATT_HEADS, ATT_V_DIM + SUM_ROWS, s), BF16),
                        pltpu.VMEM((chains, t), F32),
                        pltpu.VMEM((chains, ATT_V_DIM + SUM_ROWS, t), F32)],
        compiler_params=pltpu.CompilerParams(
            dimension_semantics=("arbitrary", "arbitrary"),
            vmem_limit_bytes=VMEM_LIMIT),
        name="diff_attention",
    )(qkv, qkv, qkv, gates, bias_tiles, lam_qk_l, subln_g_l.reshape(1, ATT_V_DIM))


def _segment_ones():
    idx = np.arange(RWKV_WIDTH) // RWKV_HEAD
    return jnp.asarray((idx[:, None] == idx[None, :]).astype(np.float32), dtype=BF16)


CHUNK = 64
CHUNK_GROUP = 8


def _dot_tn(a, b):
    return lax.dot_general(a, b, (((0,), (0,)), ((), ())), preferred_element_type=F32)


def _rwkv_kernel(p_ref, halo_ref, mu_ref, w0_ref, wup_ref, a0_ref, aup_ref, kkw_ref, ka_ref,
                 rk_ref, seg_ref, y_o, bonus_o, r_s, k_s, v_s, lw_s, a_s, b_s, state_ref):
    i = pl.program_id(1)

    @pl.when(i == 0)
    def _():
        state_ref[...] = jnp.zeros(state_ref.shape, F32)

    p = p_ref[0]
    tm = p.shape[0]
    last_prev = jnp.where(i > 0, halo_ref[0][SUBLANES - 1:SUBLANES, :], 0.0)
    row = lax.broadcasted_iota(jnp.int32, p.shape, 0)
    prev = jnp.where(row == 0, last_prev, pltpu.roll(p, 1, axis=0))
    p = p + (prev - p) * mu_ref[...]
    w3 = 3 * RWKV_WIDTH
    r = p[:, 0:RWKV_WIDTH]
    k = p[:, RWKV_WIDTH:2 * RWKV_WIDTH]
    v = p[:, 2 * RWKV_WIDTH:w3]
    wd = p[:, w3:w3 + DECAY_LORA]
    ad = p[:, w3 + DECAY_LORA:w3 + DECAY_LORA + ICLR_LORA]
    seg = seg_ref[...]

    x = -(w0_ref[...] + _dot_f32(jnp.tanh(wd), wup_ref[...]))
    softplus = jnp.maximum(x, 0.0) + jnp.log(1.0 + jnp.exp(-jnp.abs(x)))
    w = -softplus - 0.5
    a = _sigmoid(a0_ref[...] + _dot_f32(ad, aup_ref[...]))
    kk = k * kkw_ref[...]
    norm = jnp.sqrt(_seg_sum(kk * kk, seg))
    kk = kk / jnp.maximum(norm, 1e-12)
    k = k * (1.0 + (a - 1.0) * ka_ref[...])
    bonus_o[0] = (_seg_sum(r * k * rk_ref[...], seg) * v).astype(bonus_o.dtype)

    r_s[...] = r
    k_s[...] = k
    v_s[...] = v
    lw_s[...] = -jnp.exp(w)
    a_s[...] = -kk
    b_s[...] = kk * a

    c = CHUNK
    heads = range(RWKV_HEADS)
    lane_w = lax.broadcasted_iota(jnp.int32, (1, RWKV_WIDTH), 1)
    head_mask = [lane_w // RWKV_HEAD == h for h in heads]
    rr = lax.broadcasted_iota(jnp.int32, (2 * c, 2 * c), 0)
    cc = lax.broadcasted_iota(jnp.int32, (2 * c, 2 * c), 1)
    causal = (cc % c) < jnp.where(rr < c, rr, rr - c + 1)
    top_right = lax.broadcasted_iota(jnp.int32, (c, 2 * c), 1) >= c
    r64 = lax.broadcasted_iota(jnp.int32, (c, c), 0)
    c64 = lax.broadcasted_iota(jnp.int32, (c, c), 1)
    tril_ones = (c64 <= r64).astype(BF16)
    eye = (r64 == c64).astype(F32)
    rw_ = lax.broadcasted_iota(jnp.int32, (RWKV_WIDTH, RWKV_WIDTH), 0)
    cw_ = lax.broadcasted_iota(jnp.int32, (RWKV_WIDTH, RWKV_WIDTH), 1)
    block_diag = (rw_ // RWKV_HEAD) == (cw_ // RWKV_HEAD)
    diag = rw_ == cw_
    zero = jnp.zeros((c, RWKV_WIDTH), F32)
    bf = lambda t: t.astype(BF16)

    group = min(CHUNK_GROUP, tm // c)
    pick = lambda parts: functools.reduce(
        lambda acc, hp: jnp.where(head_mask[hp[0]], hp[1], acc), enumerate(parts), zero)

    def chunks(n, carry):
        gs = range(group)
        streams = [(g, h) for g in gs for h in heads]
        rows = [pl.ds(pl.multiple_of((n * group + g) * c, c), c) for g in gs]
        rc, kc, vc, lw, ac, bc = ([ref[rows[g], :] for g in gs] for ref in (r_s, k_s, v_s, lw_s, a_s, b_s))
        cum = [_seg_sum_left(tril_ones, lw[g]) for g in gs]
        gam = [jnp.exp(cum[g]) for g in gs]
        gam_inv = [jnp.exp(-cum[g]) for g in gs]
        at = [ac[g] * jnp.exp(cum[g] - lw[g]) for g in gs]
        rt = [rc[g] * gam[g] for g in gs]
        bt = [bc[g] * gam_inv[g] for g in gs]
        kt = [kc[g] * gam_inv[g] for g in gs]
        g_end = [gam[g][c - 1:c, :] for g in gs]
        left = [jnp.concatenate([at[g], rt[g]], axis=0) for g in gs]
        right = [bf(jnp.concatenate([bt[g], kt[g]], axis=0)) for g in gs]

        gm = {(g, h): jnp.where(causal, _dot_nt(bf(jnp.where(head_mask[h], left[g], 0.0)), right[g]), 0.0)
              for g, h in streams}
        npow = {s: gm[s][0:c, 0:c] for s in streams}
        t_inv = {s: eye + npow[s] for s in streams}
        for _ in range(5):
            npow = {s: _dot(bf(npow[s]), bf(npow[s])) for s in streams}
            t_inv = {s: t_inv[s] + _dot(bf(t_inv[s]), bf(npow[s])) for s in streams}

        zero_v = [bf(jnp.concatenate([zero, vc[g]], axis=0)) for g in gs]
        q1_pre = {(g, h): _dot(bf(jnp.where(top_right, gm[g, h][0:c, :], 0.0)), zero_v[g])
                  for g, h in streams}
        sol = {(g, h): _dot(bf(t_inv[g, h]), bf(jnp.concatenate([at[g], q1_pre[g, h]], axis=1)))
               for g, h in streams}
        low = {(g, h): _dot(bf(gm[g, h][c:2 * c, :]),
                            bf(jnp.concatenate([sol[g, h], jnp.concatenate([zero, vc[g]], axis=1)], axis=0)))
               for g, h in streams}

        a_new = [pick([sol[g, h][:, :RWKV_WIDTH] for h in heads]) for g in gs]
        q1 = [pick([sol[g, h][:, RWKV_WIDTH:] for h in heads]) for g in gs]
        r_new = [rt[g] + pick([low[g, h][:, :RWKV_WIDTH] for h in heads]) for g in gs]
        q2 = [pick([low[g, h][:, RWKV_WIDTH:] for h in heads]) for g in gs]
        m_bd = [jnp.where(block_diag, _dot_tn(bf(bt[g] * g_end[g]), bf(a_new[g])), 0.0)
                + jnp.where(diag, g_end[g], 0.0) for g in gs]
        j_bd = [jnp.where(block_diag,
                          _dot_tn(bf(jnp.concatenate([bt[g] * g_end[g], kt[g] * g_end[g]], axis=0)),
                                  bf(jnp.concatenate([q1[g], vc[g]], axis=0))), 0.0) for g in gs]

        state = state_ref[...]
        for g in gs:
            y_o[0, rows[g], :] = (_dot_f32(r_new[g], state) + q2[g]).astype(y_o.dtype)
            state = _dot_f32(m_bd[g], state) + j_bd[g]
        state_ref[...] = state
        return carry

    lax.fori_loop(0, tm // (c * group), chunks, 0)


def _seg_sum_left(ones_bf16, x):
    hi, lo = _split2(x)
    return _dot(ones_bf16, hi) + _dot(ones_bf16, lo)


def _rwkv_mix(rw, mu, w0, w_up, a0, a_up, k_k, k_a, r_k):
    b, s, _ = rw.shape
    tm = min(TOK_TILE, s)
    row = lambda a, n: a.reshape(1, n).astype(F32)
    full = lambda shape: pl.BlockSpec(shape, lambda bi, i: (0,) * len(shape))
    out_spec = pl.BlockSpec((1, tm, RWKV_WIDTH), lambda bi, i: (bi, i, 0))
    out_sds = jax.ShapeDtypeStruct((b, s, RWKV_WIDTH), BF16)
    halo_blocks = tm // SUBLANES
    tile_scratch = pltpu.VMEM((tm, RWKV_WIDTH), F32)
    return pl.pallas_call(
        _rwkv_kernel,
        grid=(b, s // tm),
        in_specs=[pl.BlockSpec((1, tm, RWKV_SHIFT_COLS), lambda bi, i: (bi, i, 0)),
                  pl.BlockSpec((1, SUBLANES, RWKV_SHIFT_COLS),
                               lambda bi, i: (bi, jnp.maximum(i * halo_blocks - 1, 0), 0)),
                  full((1, RWKV_SHIFT_COLS)), full((1, RWKV_WIDTH)), full((DECAY_LORA, RWKV_WIDTH)),
                  full((1, RWKV_WIDTH)), full((ICLR_LORA, RWKV_WIDTH)), full((1, RWKV_WIDTH)),
                  full((1, RWKV_WIDTH)), full((1, RWKV_WIDTH)), full((RWKV_WIDTH, RWKV_WIDTH))],
        out_specs=[out_spec] * 2,
        out_shape=[out_sds] * 2,
        scratch_shapes=[tile_scratch] * 6 + [pltpu.VMEM((RWKV_WIDTH, RWKV_WIDTH), F32)],
        compiler_params=pltpu.CompilerParams(dimension_semantics=("arbitrary", "arbitrary"),
                                             vmem_limit_bytes=VMEM_LIMIT),
        name="rwkv_mix",
    )(rw, rw, row(mu, RWKV_SHIFT_COLS), row(w0, RWKV_WIDTH), w_up.astype(F32), row(a0, RWKV_WIDTH),
      a_up.astype(F32), row(k_k, RWKV_WIDTH), row(k_a, RWKV_WIDTH), row(r_k, RWKV_WIDTH), _segment_ones())


def _shift_rows(u, halo_u, shift):
    rolled = pltpu.roll(u, shift, axis=0)
    head = jnp.where(lax.broadcasted_iota(jnp.int32, halo_u.shape, 0) < shift,
                     pltpu.roll(halo_u, shift, axis=0), rolled[0:SUBLANES])
    return jnp.concatenate([head, rolled[SUBLANES:]], axis=0)


def _out_kernel(x_ref, att_ref, gate_ref, halo_ref, y_ref, bonus_ref, cw_ref, lng_ref, lnb_ref, seg_ref,
                wout_ref, fg_ref, o_ref, *, final):
    i = pl.program_id(1)
    c0 = ATT_WIDTH
    cols = lambda ref, rows, j, width=CONV_WIDTH: ref[0, rows, c0 + j * CONV_WIDTH:c0 + j * CONV_WIDTH + width].astype(F32)
    every = slice(None)
    cb, cc, ch, zc = (cols(gate_ref, every, j) for j in range(4))
    zr = cols(gate_ref, every, 4, RWKV_WIDTH)
    u = cc * ch
    halo_u = (cols(halo_ref, every, 1) * cols(halo_ref, every, 2))[HALO_ROWS - SUBLANES:HALO_ROWS]
    halo_u = jnp.where(i > 0, halo_u, 0.0)
    cw = cw_ref[...]
    conv = (cw[0:1] * _shift_rows(u, halo_u, 2) + cw[1:2] * _shift_rows(u, halo_u, 1)) + cw[2:3] * u
    cv = (cb * conv) * (zc * _sigmoid(zc))

    seg = seg_ref[...]
    y = y_ref[0].astype(F32)
    inv_n = 1.0 / RWKV_HEAD
    mean = _seg_sum(y, seg) * inv_n
    d = y - mean
    var = _seg_sum(d * d, seg) * inv_n
    yn = d * lax.rsqrt(var + GN_EPS) * lng_ref[...] + lnb_ref[...]
    rw = (yn + bonus_ref[0].astype(F32)) * (zr * _sigmoid(zr))

    upd = (_dot(att_ref[0], wout_ref[0:ATT_WIDTH, :])
           + _dot(cv.astype(BF16), wout_ref[ATT_WIDTH:ATT_WIDTH + CONV_WIDTH, :])
           + _dot(rw.astype(BF16), wout_ref[ATT_WIDTH + CONV_WIDTH:, :]))
    xn = x_ref[0] + upd
    if final:
        ms = jnp.mean(xn * xn, axis=-1, keepdims=True)
        xn = xn * lax.rsqrt(ms + NORM_EPS) * fg_ref[...]
    o_ref[0] = xn


def _out_proj(x, att, gates, y, bonus, conv_w_l, lnx_g_l, lnx_b_l, w_out_bf16, final_g, final):
    b, s, _ = x.shape
    tm = min(TOK_TILE, s)
    row = lambda a, n: a.reshape(1, n).astype(F32)
    full = lambda shape: pl.BlockSpec(shape, lambda bi, i: (0,) * len(shape))
    tok = lambda n: pl.BlockSpec((1, tm, n), lambda bi, i: (bi, i, 0))
    halo_blocks = tm // HALO_ROWS
    return pl.pallas_call(
        functools.partial(_out_kernel, final=final),
        grid=(b, s // tm),
        in_specs=[tok(D_MODEL), tok(ATT_WIDTH), tok(GATE_COLS),
                  pl.BlockSpec((1, HALO_ROWS, GATE_COLS),
                               lambda bi, i: (bi, jnp.maximum(i * halo_blocks - 1, 0), 0)),
                  tok(RWKV_WIDTH), tok(RWKV_WIDTH),
                  full((CONV_K, CONV_WIDTH)), full((1, RWKV_WIDTH)), full((1, RWKV_WIDTH)),
                  full((RWKV_WIDTH, RWKV_WIDTH)), full((D_MODEL, D_MODEL)), full((1, D_MODEL))],
        out_specs=tok(D_MODEL),
        out_shape=jax.ShapeDtypeStruct((b, s, D_MODEL), F32),
        compiler_params=pltpu.CompilerParams(dimension_semantics=("arbitrary", "arbitrary"),
                                             vmem_limit_bytes=VMEM_LIMIT),
        name="out_proj",
    )(x, att, gates, gates, y, bonus, conv_w_l.astype(F32),
      row(lnx_g_l, RWKV_WIDTH), row(lnx_b_l, RWKV_WIDTH), _segment_ones(), w_out_bf16,
      row(final_g, D_MODEL))


def kernel(x, norm_g, w_in, w_out, final_norm_g, rel_bias, lam_qk, subln_g, conv_w, rwkv_mu, w0, w_up,
           a0, a_up, k_k, k_a, r_k, lnx_g, lnx_b):
    b, s, _ = x.shape
    tile = min(ATT_TILE, s)
    bias = _bias_tiles(rel_bias, tile)
    w_in_b = w_in.astype(BF16)
    w_out_b = w_out.astype(BF16)
    x = x.astype(F32)
    for l in range(DEPTH):
        lambda_init = 0.8 - 0.6 * math.exp(-0.3 * l)
        qkv, gates, rw = _in_proj(x.reshape(b * s, D_MODEL), norm_g[l].astype(F32), w_in_b[l])
        qkv = qkv.reshape(b, s, QKV_COLS)
        gates = gates.reshape(b, s, GATE_COLS)
        rw = rw.reshape(b, s, RWKV_SHIFT_COLS)
        att = _attention(qkv, gates, bias, lam_qk[l].astype(F32), subln_g[l].astype(F32), lambda_init)
        y, bonus = _rwkv_mix(rw, rwkv_mu[l], w0[l], w_up[l], a0[l], a_up[l], k_k[l], k_a[l], r_k[l])
        x = _out_proj(x, att, gates, y, bonus, conv_w[l], lnx_g[l], lnx_b[l], w_out_b[l],
                      final_norm_g, final=(l == DEPTH - 1))
    return x
```

```python
import functools
import math

import numpy as np
import jax
import jax.numpy as jnp
from jax import lax
from jax.experimental import pallas as pl
from jax.experimental.pallas import tpu as pltpu

F32 = jnp.float32
BF16 = jnp.bfloat16

D_MODEL = 1024
DEPTH = 4
ATT_HEADS = 4
ATT_QK_DIM = 64
ATT_V_DIM = 128
ATT_WIDTH = ATT_HEADS * ATT_V_DIM
ATT_QK_COLS = ATT_HEADS * 2 * ATT_QK_DIM
CONV_WIDTH = 256
CONV_K = 3
RWKV_HEAD = 64
RWKV_WIDTH = 256
RWKV_HEADS = 4
DECAY_LORA = 64
ICLR_LORA = 64
RWKV_SHIFT_COLS = 3 * RWKV_WIDTH + DECAY_LORA + ICLR_LORA
IN_COLS = 2 * ATT_QK_COLS + 2 * ATT_WIDTH + 4 * CONV_WIDTH + RWKV_SHIFT_COLS + RWKV_WIDTH
N_BUCKETS = 32
MAX_DISTANCE = 128
NEG_INF = -1e30
NORM_EPS = 1e-6
SUBLN_EPS = 1e-5
GN_EPS = 64e-5

QKV_COLS = 2 * ATT_QK_COLS + ATT_WIDTH
GATE_COLS = ATT_WIDTH + 4 * CONV_WIDTH + RWKV_WIDTH
_OFF_ZATT = QKV_COLS
_OFF_RW = _OFF_ZATT + ATT_WIDTH + 4 * CONV_WIDTH
_OFF_ZRW = _OFF_RW + RWKV_SHIFT_COLS

LOG2E = math.log2(math.e)
Q_SCALE = ATT_QK_DIM ** -0.5 * LOG2E

LANES = 128
SUBLANES = 8
HALO_ROWS = 16
VMEM_LIMIT = 56 * 1024 * 1024

ATT_TILE = 256
SCORE_LOOKAHEAD = 6
FAR_GROUP = 4
SUM_ROWS = 16
TOK_TILE = 512


def _dot(a, b):
    return jnp.dot(a, b, preferred_element_type=F32)


def _dot_nt(a, b):
    return lax.dot_general(a, b, (((1,), (1,)), ((), ())), preferred_element_type=F32)


def _split2(x):
    hi = x.astype(BF16)
    lo = (x - hi.astype(F32)).astype(BF16)
    return hi, lo


def _dot_f32(a, b):
    ah, al = _split2(a)
    bh, bl = _split2(b)
    return _dot(ah, bh) + (_dot(ah, bl) + _dot(al, bh))


def _seg_sum(x, seg):
    hi, lo = _split2(x)
    return _dot(hi, seg) + _dot(lo, seg)


def _sigmoid(x):
    return 1.0 / (1.0 + jnp.exp(-x))


def _bucket_tiles(tile):
    key = np.arange(tile)[:, None]
    query = np.arange(tile)[None, :]
    out = []
    for off in (0, tile):
        dist = query - key + off
        n = np.maximum(dist, 0)
        max_exact = N_BUCKETS // 2
        nf = np.maximum(n, 1).astype(np.float32)
        large = max_exact + (np.log(nf / np.float32(max_exact)) / np.float32(math.log(MAX_DISTANCE / max_exact))
                             * np.float32(N_BUCKETS - max_exact)).astype(np.int32)
        large = np.minimum(large, N_BUCKETS - 1)
        bucket = np.where(n < max_exact, n, large)
        out.append(np.where(dist >= 0, bucket, -1))
    return np.stack(out).astype(np.int32)


def _bias_kernel(tab_ref, bucket_ref, o_ref):
    h = pl.program_id(0)
    bk = bucket_ref[0]
    far = tab_ref[N_BUCKETS - 1, h]
    val = jnp.zeros(bk.shape, F32)
    for i in range(N_BUCKETS - 1):
        val = jnp.where(bk == i, (tab_ref[i, h] - far) * LOG2E, val)
    o_ref[0, 0] = jnp.where(bk < 0, NEG_INF, val)


def _bias_tiles(rel_bias, tile):
    buckets = jnp.asarray(_bucket_tiles(tile))
    return pl.pallas_call(
        _bias_kernel,
        grid=(ATT_HEADS, 2),
        in_specs=[pl.BlockSpec(memory_space=pltpu.SMEM),
                  pl.BlockSpec((1, tile, tile), lambda h, d: (d, 0, 0))],
        out_specs=pl.BlockSpec((1, 1, tile, tile), lambda h, d: (h, d, 0, 0)),
        out_shape=jax.ShapeDtypeStruct((ATT_HEADS, 2, tile, tile), F32),
        name="bias_tiles",
    )(rel_bias.astype(F32), buckets)


def _col_chunks(start, stop, width=512):
    c = start
    while c < stop:
        yield c, min(width, stop - c)
        c += width


def _in_proj_kernel(x_ref, g_ref, w_ref, qkv_ref, gate_ref, rw_ref):
    x = x_ref[...]
    ms = jnp.mean(x * x, axis=-1, keepdims=True)
    h = (x * lax.rsqrt(ms + NORM_EPS) * g_ref[...]).astype(BF16)
    for c, n in _col_chunks(0, QKV_COLS, ATT_QK_COLS):
        acc = _dot(h, w_ref[:, c:c + n])
        if c < ATT_QK_COLS:
            acc = acc * Q_SCALE
        qkv_ref[:, c:c + n] = acc.astype(BF16)
    for c, n in _col_chunks(_OFF_ZATT, _OFF_RW):
        gate_ref[:, c - _OFF_ZATT:c - _OFF_ZATT + n] = _dot(h, w_ref[:, c:c + n]).astype(BF16)
    for c, n in _col_chunks(_OFF_RW, _OFF_ZRW):
        rw_ref[:, c - _OFF_RW:c - _OFF_RW + n] = _dot(h, w_ref[:, c:c + n])
    g0 = _OFF_RW - _OFF_ZATT
    gate_ref[:, g0:g0 + RWKV_WIDTH] = _dot(h, w_ref[:, _OFF_ZRW:IN_COLS]).astype(BF16)


def _in_proj(x2d, g, w_bf16):
    m = x2d.shape[0]
    tm = TOK_TILE
    return pl.pallas_call(
        _in_proj_kernel,
        grid=(m // tm,),
        in_specs=[pl.BlockSpec((tm, D_MODEL), lambda i: (i, 0)),
                  pl.BlockSpec((1, D_MODEL), lambda i: (0, 0)),
                  pl.BlockSpec((D_MODEL, IN_COLS), lambda i: (0, 0))],
        out_specs=[pl.BlockSpec((tm, QKV_COLS), lambda i: (i, 0)),
                   pl.BlockSpec((tm, GATE_COLS), lambda i: (i, 0)),
                   pl.BlockSpec((tm, RWKV_SHIFT_COLS), lambda i: (i, 0))],
        out_shape=[jax.ShapeDtypeStruct((m, QKV_COLS), BF16),
                   jax.ShapeDtypeStruct((m, GATE_COLS), BF16),
                   jax.ShapeDtypeStruct((m, RWKV_SHIFT_COLS), F32)],
        compiler_params=pltpu.CompilerParams(dimension_semantics=("arbitrary",),
                                             vmem_limit_bytes=VMEM_LIMIT),
        name="in_proj",
    )(x2d, g.reshape(1, D_MODEL), w_bf16)


def _attn_kernel(q_ref, k_ref, v_ref, z_ref, bias_ref, lq_ref, g_ref, o_ref,
                 qz_ref, vt_ref, m_ref, acc_ref, *, tile, seq, lambda_init):
    i = pl.program_id(1)
    chains = 2 * ATT_HEADS
    dv = ATT_V_DIM

    @pl.when(i == 0)
    def _():
        def tr(c, carry):
            start = pl.multiple_of(c * tile, tile)
            for h in range(ATT_HEADS):
                blk = v_ref[0, pl.ds(start, tile), h * LANES:(h + 1) * LANES]
                vt_ref[h, 0:dv, pl.ds(start, tile)] = blk.astype(F32).T.astype(BF16)
                vt_ref[h, dv:dv + SUM_ROWS, pl.ds(start, tile)] = jnp.ones((SUM_ROWS, tile), BF16)
            return carry
        lax.fori_loop(0, seq // tile, tr, 0)

    q = q_ref[0]
    lane = lax.broadcasted_iota(jnp.int32, (tile, LANES), 1)
    for h in range(ATT_HEADS):
        qh = q[:, h * LANES:(h + 1) * LANES]
        qz_ref[2 * h] = jnp.where(lane < ATT_QK_DIM, qh, jnp.zeros_like(qh))
        qz_ref[2 * h + 1] = jnp.where(lane >= ATT_QK_DIM, qh, jnp.zeros_like(qh))

    m_ref[...] = jnp.full(m_ref.shape, NEG_INF, F32)
    acc_ref[...] = jnp.zeros(acc_ref.shape, F32)

    def update(jobs):
        steps = [(pl.multiple_of(j * tile, tile), bias_idx, c) for j, bias_idx in jobs for c in range(chains)]

        def scores(start, bias_idx, c):
            h = c // 2
            kb = k_ref[0, pl.ds(start, tile), h * LANES:(h + 1) * LANES]
            s = _dot_nt(kb, qz_ref[c])
            if bias_idx is not None:
                s = s + bias_ref[h, bias_idx]
            return s

        def softmax(start, c, s):
            m_prev = m_ref[c:c + 1, :]
            m_next = jnp.maximum(m_prev, jnp.max(s, axis=0, keepdims=True))
            alpha = jnp.exp2(m_prev - m_next)
            p = jnp.exp2(s - m_next)
            m_ref[c:c + 1, :] = m_next
            vtb = vt_ref[c // 2, :, pl.ds(start, tile)]
            return alpha, _dot(vtb, p.astype(BF16))

        def accumulate(c, alpha, pv):
            acc_ref[c] = alpha * acc_ref[c] + pv

        ahead = [scores(*st) for st in steps[:SCORE_LOOKAHEAD]]
        pending = None
        for n, (start, _, c) in enumerate(steps):
            s = ahead.pop(0)
            if n + SCORE_LOOKAHEAD < len(steps):
                ahead.append(scores(*steps[n + SCORE_LOOKAHEAD]))
            alpha, pv = softmax(start, c, s)
            if pending is not None:
                accumulate(*pending)
            pending = (c, alpha, pv)
        accumulate(*pending)

    n_far = jnp.maximum(i - 1, 0)
    fg = FAR_GROUP

    def far_body(trip, carry):
        update([(fg * trip + g, None) for g in range(fg)])
        return carry

    lax.fori_loop(0, n_far // fg, far_body, 0)

    @pl.when(i == 0)
    def _():
        update([(i, 0)])

    for rem in range(fg):
        @pl.when(jnp.logical_and(i >= 1, n_far % fg == rem))
        def _():
            update([(i - 1 - rem + g, None) for g in range(rem)] + [(i - 1, 1), (i, 0)])

    lq = lq_ref[...]
    lam = (jnp.exp(jnp.sum(lq[0:1] * lq[1:2], axis=1, keepdims=True))
           - jnp.exp(jnp.sum(lq[2:3] * lq[3:4], axis=1, keepdims=True)) + lambda_init)
    for h in range(ATT_HEADS):
        c = 2 * h
        ot = (acc_ref[c, 0:dv, :] / acc_ref[c, dv:dv + 1, :]
              - lam * (acc_ref[c + 1, 0:dv, :] / acc_ref[c + 1, dv:dv + 1, :]))
        o = ot.T
        ms = jnp.mean(o * o, axis=-1, keepdims=True)
        o = o * lax.rsqrt(ms + SUBLN_EPS) * g_ref[...] * (1.0 - lambda_init)
        z = z_ref[0, :, h * LANES:(h + 1) * LANES].astype(F32)
        o_ref[0, :, h * LANES:(h + 1) * LANES] = (o * (z * _sigmoid(z))).astype(o_ref.dtype)


def _attention(qkv, gates, bias_tiles, lam_qk_l, subln_g_l, lambda_init):
    b, s, _ = qkv.shape
    t = bias_tiles.shape[-1]
    kern = functools.partial(_attn_kernel, tile=t, seq=s, lambda_init=lambda_init)
    chains = 2 * ATT_HEADS
    return pl.pallas_call(
        kern,
        grid=(b, s // t),
        in_specs=[pl.BlockSpec((1, t, ATT_QK_COLS), lambda bi, i: (bi, i, 0)),
                  pl.BlockSpec((1, s, ATT_QK_COLS), lambda bi, i: (bi, 0, 1)),
                  pl.BlockSpec((1, s, ATT_WIDTH), lambda bi, i: (bi, 0, 2)),
                  pl.BlockSpec((1, t, ATT_WIDTH), lambda bi, i: (bi, i, 0)),
                  pl.BlockSpec((ATT_HEADS, 2, t, t), lambda bi, i: (0, 0, 0, 0)),
                  pl.BlockSpec((4, ATT_QK_DIM), lambda bi, i: (0, 0)),
                  pl.BlockSpec((1, ATT_V_DIM), lambda bi, i: (0, 0))],
        out_specs=pl.BlockSpec((1, t, ATT_WIDTH), lambda bi, i: (bi, i, 0)),
        out_shape=jax.ShapeDtypeStruct((b, s, ATT_WIDTH), BF16),
        scratch_shapes=[pltpu.VMEM((chains, t, LANES), BF16),
                        pltpu.VMEM((ATT_HEADS, ATT_V_DIM + SUM_ROWS, s), BF16),
                        pltpu.VMEM((chains, t), F32),
                        pltpu.VMEM((chains, ATT_V_DIM + SUM_ROWS, t), F32)],
        compiler_params=pltpu.CompilerParams(
            dimension_semantics=("arbitrary", "arbitrary"),
            vmem_limit_bytes=VMEM_LIMIT),
        name="diff_attention",
    )(qkv, qkv, qkv, gates, bias_tiles, lam_qk_l, subln_g_l.reshape(1, ATT_V_DIM))


def _segment_ones():
    idx = np.arange(RWKV_WIDTH) // RWKV_HEAD
    return jnp.asarray((idx[:, None] == idx[None, :]).astype(np.float32), dtype=BF16)


CHUNK = 64
CHUNK_GROUP = 8


def _dot_tn(a, b):
    return lax.dot_general(a, b, (((0,), (0,)), ((), ())), preferred_element_type=F32)


def _rwkv_kernel(p_ref, halo_ref, mu_ref, w0_ref, wup_ref, a0_ref, aup_ref, kkw_ref, ka_ref,
                 rk_ref, seg_ref, y_o, bonus_o, r_s, k_s, v_s, lw_s, a_s, b_s, state_ref):
    i = pl.program_id(1)

    @pl.when(i == 0)
    def _():
        state_ref[...] = jnp.zeros(state_ref.shape, F32)

    p = p_ref[0]
    tm = p.shape[0]
    last_prev = jnp.where(i > 0, halo_ref[0][SUBLANES - 1:SUBLANES, :], 0.0)
    row = lax.broadcasted_iota(jnp.int32, p.shape, 0)
    prev = jnp.where(row == 0, last_prev, pltpu.roll(p, 1, axis=0))
    p = p + (prev - p) * mu_ref[...]
    w3 = 3 * RWKV_WIDTH
    r = p[:, 0:RWKV_WIDTH]
    k = p[:, RWKV_WIDTH:2 * RWKV_WIDTH]
    v = p[:, 2 * RWKV_WIDTH:w3]
    wd = p[:, w3:w3 + DECAY_LORA]
    ad = p[:, w3 + DECAY_LORA:w3 + DECAY_LORA + ICLR_LORA]
    seg = seg_ref[...]

    x = -(w0_ref[...] + _dot_f32(jnp.tanh(wd), wup_ref[...]))
    softplus = jnp.maximum(x, 0.0) + jnp.log(1.0 + jnp.exp(-jnp.abs(x)))
    w = -softplus - 0.5
    a = _sigmoid(a0_ref[...] + _dot_f32(ad, aup_ref[...]))
    kk = k * kkw_ref[...]
    norm = jnp.sqrt(_seg_sum(kk * kk, seg))
    kk = kk / jnp.maximum(norm, 1e-12)
    k = k * (1.0 + (a - 1.0) * ka_ref[...])
    bonus_o[0] = (_seg_sum(r * k * rk_ref[...], seg) * v).astype(bonus_o.dtype)

    r_s[...] = r
    k_s[...] = k
    v_s[...] = v
    lw_s[...] = -jnp.exp(w)
    a_s[...] = -kk
    b_s[...] = kk * a

    c = CHUNK
    row_c = lax.broadcasted_iota(jnp.int32, (c, RWKV_WIDTH), 0)
    col_c = lax.broadcasted_iota(jnp.int32, (c, RWKV_WIDTH), 1) % c
    strict = col_c < row_c
    causal = col_c <= row_c
    eye = (col_c == row_c).astype(F32)
    r64 = lax.broadcasted_iota(jnp.int32, (c, c), 0)
    c64 = lax.broadcasted_iota(jnp.int32, (c, c), 1)
    tril_ones = (c64 <= r64).astype(BF16)
    rw_ = lax.broadcasted_iota(jnp.int32, (RWKV_WIDTH, RWKV_WIDTH), 0)
    cw_ = lax.broadcasted_iota(jnp.int32, (RWKV_WIDTH, RWKV_WIDTH), 1)
    block_diag = (rw_ // RWKV_HEAD) == (cw_ // RWKV_HEAD)
    diag = rw_ == cw_
    bf = lambda t: t.astype(BF16)
    by_head = lambda x: bf(jnp.where(block_diag, jnp.concatenate([x] * RWKV_HEADS, axis=0), 0.0))

    group = min(CHUNK_GROUP, tm // c)

    def chunks(n, carry):
        gs = range(group)
        rows =[pl.ds(pl.multiple_of((n * group + g) * c, c), c) for g in gs]
        rc, kc, vc, lw, ac, bc = ([ref[rows[g], :] for g in gs] for ref in (r_s, k_s, v_s, lw_s, a_s, b_s))
        cum = [_seg_sum_left(tril_ones, lw[g]) for g in gs]
        gam = [jnp.exp(cum[g]) for g in gs]
        gam_inv = [jnp.exp(-cum[g]) for g in gs]
        at = [ac[g] * jnp.exp(cum[g] - lw[g]) for g in gs]
        rt = [rc[g] * gam[g] for g in gs]
        bt = [bc[g] * gam_inv[g] for g in gs]
        kt = [kc[g] * gam_inv[g] for g in gs]
        g_end = [gam[g][c - 1:c, :] for g in gs]
        left = [bf(jnp.concatenate([at[g], rt[g]], axis=0)) for g in gs]
        prod_b = [_dot_nt(left[g], by_head(bt[g])) for g in gs]
        prod_k = [_dot_nt(left[g], by_head(kt[g])) for g in gs]
        npow = [jnp.where(strict, prod_b[g][0:c], 0.0) for g in gs]
        ntr = [jnp.where(strict, prod_k[g][0:c], 0.0) for g in gs]
        lb = [jnp.where(causal, prod_b[g][c:2 * c], 0.0) for g in gs]
        lk = [jnp.where(causal, prod_k[g][c:2 * c], 0.0) for g in gs]
        t_inv = [eye + npow[g] for g in gs]
        npow = [_dot(bf(npow[g]), by_head(npow[g])) for g in gs]
        for level in range(1, 6):
            lhs = [jnp.concatenate([t_inv[g], npow[g]], axis=0) if level < 5 else t_inv[g] for g in gs]
            both = [_dot(bf(lhs[g]), by_head(npow[g])) for g in gs]
            t_inv = [t_inv[g] + both[g][0:c] for g in gs]
            if level < 5:
                npow = [both[g][c:2 * c] for g in gs]
        by_v = [_dot(bf(jnp.concatenate([ntr[g], lk[g]], axis=0)), by_head(vc[g])) for g in gs]
        a_new = [_dot(bf(t_inv[g]), by_head(at[g])) for g in gs]
        q1 = [_dot(bf(t_inv[g]), by_head(by_v[g][0:c])) for g in gs]
        r_new = [rt[g] + _dot(bf(lb[g]), by_head(a_new[g])) for g in gs]
        q2 = [_dot(bf(lb[g]), by_head(q1[g])) + by_v[g][c:2 * c] for g in gs]
        m_bd = [jnp.where(block_diag, _dot_tn(bf(bt[g] * g_end[g]), bf(a_new[g])), 0.0)
                + jnp.where(diag, g_end[g], 0.0) for g in gs]
        j_bd = [jnp.where(block_diag,
                          _dot_tn(bf(jnp.concatenate([bt[g] * g_end[g], kt[g] * g_end[g]], axis=0)),
                                  bf(jnp.concatenate([q1[g], vc[g]], axis=0))), 0.0) for g in gs]

        state = state_ref[...]
        for g in gs:
            y_o[0, rows[g], :] = (_dot_f32(r_new[g], state) + q2[g]).astype(y_o.dtype)
            state = _dot_f32(m_bd[g], state) + j_bd[g]
        state_ref[...] = state
        return carry

    lax.fori_loop(0, tm // (c * group), chunks, 0)


def _seg_sum_left(ones_bf16, x):
    hi, lo = _split2(x)
    return _dot(ones_bf16, hi) + _dot(ones_bf16, lo)


def _rwkv_mix(rw, mu, w0, w_up, a0, a_up, k_k, k_a, r_k):
    b, s, _ = rw.shape
    tm = min(TOK_TILE, s)
    row = lambda a, n: a.reshape(1, n).astype(F32)
    full = lambda shape: pl.BlockSpec(shape, lambda bi, i: (0,) * len(shape))
    out_spec = pl.BlockSpec((1, tm, RWKV_WIDTH), lambda bi, i: (bi, i, 0))
    out_sds = jax.ShapeDtypeStruct((b, s, RWKV_WIDTH), BF16)
    halo_blocks = tm // SUBLANES
    tile_scratch = pltpu.VMEM((tm, RWKV_WIDTH), F32)
    return pl.pallas_call(
        _rwkv_kernel,
        grid=(b, s // tm),
        in_specs=[pl.BlockSpec((1, tm, RWKV_SHIFT_COLS), lambda bi, i: (bi, i, 0)),
                  pl.BlockSpec((1, SUBLANES, RWKV_SHIFT_COLS),
                               lambda bi, i: (bi, jnp.maximum(i * halo_blocks - 1, 0), 0)),
                  full((1, RWKV_SHIFT_COLS)), full((1, RWKV_WIDTH)), full((DECAY_LORA, RWKV_WIDTH)),
                  full((1, RWKV_WIDTH)), full((ICLR_LORA, RWKV_WIDTH)), full((1, RWKV_WIDTH)),
                  full((1, RWKV_WIDTH)), full((1, RWKV_WIDTH)), full((RWKV_WIDTH, RWKV_WIDTH))],
        out_specs=[out_spec] * 2,
        out_shape=[out_sds] * 2,
        scratch_shapes=[tile_scratch] * 6 + [pltpu.VMEM((RWKV_WIDTH, RWKV_WIDTH), F32)],
        compiler_params=pltpu.CompilerParams(dimension_semantics=("arbitrary", "arbitrary"),
                                             vmem_limit_bytes=VMEM_LIMIT),
        name="rwkv_mix",
    )(rw, rw, row(mu, RWKV_SHIFT_COLS), row(w0, RWKV_WIDTH), w_up.astype(F32), row(a0, RWKV_WIDTH),
      a_up.astype(F32), row(k_k, RWKV_WIDTH), row(k_a, RWKV_WIDTH), row(r_k, RWKV_WIDTH), _segment_ones())


def _shift_rows(u, halo_u, shift):
    rolled = pltpu.roll(u, shift, axis=0)
    head = jnp.where(lax.broadcasted_iota(jnp.int32, halo_u.shape, 0) < shift,
                     pltpu.roll(halo_u, shift, axis=0), rolled[0:SUBLANES])
    return jnp.concatenate([head, rolled[SUBLANES:]], axis=0)


def _out_kernel(x_ref, att_ref, gate_ref, halo_ref, y_ref, bonus_ref, cw_ref, lng_ref, lnb_ref, seg_ref,
                wout_ref, fg_ref, o_ref, *, final):
    i = pl.program_id(1)
    c0 = ATT_WIDTH
    cols = lambda ref, rows, j, width=CONV_WIDTH: ref[0, rows, c0 + j * CONV_WIDTH:c0 + j * CONV_WIDTH + width].astype(F32)
    every = slice(None)
    cb, cc, ch, zc = (cols(gate_ref, every, j) for j in range(4))
    zr = cols(gate_ref, every, 4, RWKV_WIDTH)
    u = cc * ch
    halo_u = (cols(halo_ref, every, 1) * cols(halo_ref, every, 2))[HALO_ROWS - SUBLANES:HALO_ROWS]
    halo_u = jnp.where(i > 0, halo_u, 0.0)
    cw = cw_ref[...]
    conv = (cw[0:1] * _shift_rows(u, halo_u, 2) + cw[1:2] * _shift_rows(u, halo_u, 1)) + cw[2:3] * u
    cv = (cb * conv) * (zc * _sigmoid(zc))

    seg = seg_ref[...]
    y = y_ref[0].astype(F32)
    inv_n = 1.0 / RWKV_HEAD
    mean = _seg_sum(y, seg) * inv_n
    d = y - mean
    var = _seg_sum(d * d, seg) * inv_n
    yn = d * lax.rsqrt(var + GN_EPS) * lng_ref[...] + lnb_ref[...]
    rw = (yn + bonus_ref[0].astype(F32)) * (zr * _sigmoid(zr))

    upd = (_dot(att_ref[0], wout_ref[0:ATT_WIDTH, :])
           + _dot(cv.astype(BF16), wout_ref[ATT_WIDTH:ATT_WIDTH + CONV_WIDTH, :])
           + _dot(rw.astype(BF16), wout_ref[ATT_WIDTH + CONV_WIDTH:, :]))
    xn = x_ref[0] + upd
    if final:
        ms = jnp.mean(xn * xn, axis=-1, keepdims=True)
        xn = xn * lax.rsqrt(ms + NORM_EPS) * fg_ref[...]
    o_ref[0] = xn


def _out_proj(x, att, gates, y, bonus, conv_w_l, lnx_g_l, lnx_b_l, w_out_bf16, final_g, final):
    b, s, _ = x.shape
    tm = min(TOK_TILE, s)
    row = lambda a, n: a.reshape(1, n).astype(F32)
    full = lambda shape: pl.BlockSpec(shape, lambda bi, i: (0,) * len(shape))
    tok = lambda n: pl.BlockSpec((1, tm, n), lambda bi, i: (bi, i, 0))
    halo_blocks = tm // HALO_ROWS
    return pl.pallas_call(
        functools.partial(_out_kernel, final=final),
        grid=(b, s // tm),
        in_specs=[tok(D_MODEL), tok(ATT_WIDTH), tok(GATE_COLS),
                  pl.BlockSpec((1, HALO_ROWS, GATE_COLS),
                               lambda bi, i: (bi, jnp.maximum(i * halo_blocks - 1, 0), 0)),
                  tok(RWKV_WIDTH), tok(RWKV_WIDTH),
                  full((CONV_K, CONV_WIDTH)), full((1, RWKV_WIDTH)), full((1, RWKV_WIDTH)),
                  full((RWKV_WIDTH, RWKV_WIDTH)), full((D_MODEL, D_MODEL)), full((1, D_MODEL))],
        out_specs=tok(D_MODEL),
        out_shape=jax.ShapeDtypeStruct((b, s, D_MODEL), F32),
        compiler_params=pltpu.CompilerParams(dimension_semantics=("arbitrary", "arbitrary"),
                                             vmem_limit_bytes=VMEM_LIMIT),
        name="out_proj",
    )(x, att, gates, gates, y, bonus, conv_w_l.astype(F32),
      row(lnx_g_l, RWKV_WIDTH), row(lnx_b_l, RWKV_WIDTH), _segment_ones(), w_out_bf16,
      row(final_g, D_MODEL))


def kernel(x, norm_g, w_in, w_out, final_norm_g, rel_bias, lam_qk, subln_g, conv_w, rwkv_mu, w0, w_up,
           a0, a_up, k_k, k_a, r_k, lnx_g, lnx_b):
    b, s, _ = x.shape
    tile = min(ATT_TILE, s)
    bias = _bias_tiles(rel_bias, tile)
    w_in_b = w_in.astype(BF16)
    w_out_b = w_out.astype(BF16)
    x = x.astype(F32)
    for l in range(DEPTH):
        lambda_init = 0.8 - 0.6 * math.exp(-0.3 * l)
        qkv, gates, rw = _in_proj(x.reshape(b * s, D_MODEL), norm_g[l].astype(F32), w_in_b[l])
        qkv = qkv.reshape(b, s, QKV_COLS)
        gates = gates.reshape(b, s, GATE_COLS)
        rw = rw.reshape(b, s, RWKV_SHIFT_COLS)
        att = _attention(qkv, gates, bias, lam_qk[l].astype(F32), subln_g[l].astype(F32), lambda_init)
        y, bonus = _rwkv_mix(rw, rwkv_mu[l], w0[l], w_up[l], a0[l], a_up[l], k_k[l], k_a[l], r_k[l])
        x = _out_proj(x, att, gates, y, bonus, conv_w[l], lnx_g[l], lnx_b[l], w_out_b[l],
                      final_norm_g, final=(l == DEPTH - 1))
    return x
```

```python
import functools
import math

import numpy as np
import jax
import jax.numpy as jnp
from jax import lax
from jax.experimental import pallas as pl
from jax.experimental.pallas import tpu as pltpu

F32 = jnp.float32
BF16 = jnp.bfloat16

D_MODEL = 1024
DEPTH = 4
ATT_HEADS = 4
ATT_QK_DIM = 64
ATT_V_DIM = 128
ATT_WIDTH = ATT_HEADS * ATT_V_DIM
ATT_QK_COLS = ATT_HEADS * 2 * ATT_QK_DIM
CONV_WIDTH = 256
CONV_K = 3
RWKV_HEAD = 64
RWKV_WIDTH = 256
RWKV_HEADS = 4
DECAY_LORA = 64
ICLR_LORA = 64
RWKV_SHIFT_COLS = 3 * RWKV_WIDTH + DECAY_LORA + ICLR_LORA
IN_COLS = 2 * ATT_QK_COLS + 2 * ATT_WIDTH + 4 * CONV_WIDTH + RWKV_SHIFT_COLS + RWKV_WIDTH
N_BUCKETS = 32
MAX_DISTANCE = 128
NEG_INF = -1e30
NORM_EPS = 1e-6
SUBLN_EPS = 1e-5
GN_EPS = 64e-5

QKV_COLS = 2 * ATT_QK_COLS + ATT_WIDTH
GATE_COLS = ATT_WIDTH + 4 * CONV_WIDTH + RWKV_WIDTH
_OFF_ZATT = QKV_COLS
_OFF_RW = _OFF_ZATT + ATT_WIDTH + 4 * CONV_WIDTH
_OFF_ZRW = _OFF_RW + RWKV_SHIFT_COLS

LOG2E = math.log2(math.e)
Q_SCALE = ATT_QK_DIM ** -0.5 * LOG2E

LANES = 128
SUBLANES = 8
HALO_ROWS = 16
VMEM_LIMIT = 56 * 1024 * 1024

ATT_TILE = 256
SCORE_LOOKAHEAD = 6
FAR_GROUP = 4
SUM_ROWS = 16
TOK_TILE = 512


def _dot(a, b):
    return jnp.dot(a, b, preferred_element_type=F32)


def _dot_nt(a, b):
    return lax.dot_general(a, b, (((1,), (1,)), ((), ())), preferred_element_type=F32)


def _split2(x):
    hi = x.astype(BF16)
    lo = (x - hi.astype(F32)).astype(BF16)
    return hi, lo


def _dot_f32(a, b):
    ah, al = _split2(a)
    bh, bl = _split2(b)
    return _dot(ah, bh) + (_dot(ah, bl) + _dot(al, bh))


def _seg_sum(x, seg):
    hi, lo = _split2(x)
    return _dot(hi, seg) + _dot(lo, seg)


def _sigmoid(x):
    return 1.0 / (1.0 + jnp.exp(-x))


def _bucket_tiles(tile):
    key = np.arange(tile)[:, None]
    query = np.arange(tile)[None, :]
    out = []
    for off in (0, tile):
        dist = query - key + off
        n = np.maximum(dist, 0)
        max_exact = N_BUCKETS // 2
        nf = np.maximum(n, 1).astype(np.float32)
        large = max_exact + (np.log(nf / np.float32(max_exact)) / np.float32(math.log(MAX_DISTANCE / max_exact))
                             * np.float32(N_BUCKETS - max_exact)).astype(np.int32)
        large = np.minimum(large, N_BUCKETS - 1)
        bucket = np.where(n < max_exact, n, large)
        out.append(np.where(dist >= 0, bucket, -1))
    return np.stack(out).astype(np.int32)


def _bias_kernel(tab_ref, bucket_ref, o_ref):
    h = pl.program_id(0)
    bk = bucket_ref[0]
    far = tab_ref[N_BUCKETS - 1, h]
    val = jnp.zeros(bk.shape, F32)
    for i in range(N_BUCKETS - 1):
        val = jnp.where(bk == i, (tab_ref[i, h] - far) * LOG2E, val)
    o_ref[0, 0] = jnp.where(bk < 0, NEG_INF, val)


def _bias_tiles(rel_bias, tile):
    buckets = jnp.asarray(_bucket_tiles(tile))
    return pl.pallas_call(
        _bias_kernel,
        grid=(ATT_HEADS, 2),
        in_specs=[pl.BlockSpec(memory_space=pltpu.SMEM),
                  pl.BlockSpec((1, tile, tile), lambda h, d: (d, 0, 0))],
        out_specs=pl.BlockSpec((1, 1, tile, tile), lambda h, d: (h, d, 0, 0)),
        out_shape=jax.ShapeDtypeStruct((ATT_HEADS, 2, tile, tile), F32),
        name="bias_tiles",
    )(rel_bias.astype(F32), buckets)


def _col_chunks(start, stop, width=512):
    c = start
    while c < stop:
        yield c, min(width, stop - c)
        c += width


def _in_proj_kernel(x_ref, g_ref, w_ref, qkv_ref, gate_ref, rw_ref):
    x = x_ref[...]
    ms = jnp.mean(x * x, axis=-1, keepdims=True)
    h = (x * lax.rsqrt(ms + NORM_EPS) * g_ref[...]).astype(BF16)
    for c, n in _col_chunks(0, QKV_COLS, ATT_QK_COLS):
        acc = _dot(h, w_ref[:, c:c + n])
        if c < ATT_QK_COLS:
            acc = acc * Q_SCALE
        qkv_ref[:, c:c + n] = acc.astype(BF16)
    for c, n in _col_chunks(_OFF_ZATT, _OFF_RW):
        gate_ref[:, c - _OFF_ZATT:c - _OFF_ZATT + n] = _dot(h, w_ref[:, c:c + n]).astype(BF16)
    for c, n in _col_chunks(_OFF_RW, _OFF_ZRW):
        rw_ref[:, c - _OFF_RW:c - _OFF_RW + n] = _dot(h, w_ref[:, c:c + n])
    g0 = _OFF_RW - _OFF_ZATT
    gate_ref[:, g0:g0 + RWKV_WIDTH] = _dot(h, w_ref[:, _OFF_ZRW:IN_COLS]).astype(BF16)


def _in_proj(x2d, g, w_bf16):
    m = x2d.shape[0]
    tm = TOK_TILE
    return pl.pallas_call(
        _in_proj_kernel,
        grid=(m // tm,),
        in_specs=[pl.BlockSpec((tm, D_MODEL), lambda i: (i, 0)),
                  pl.BlockSpec((1, D_MODEL), lambda i: (0, 0)),
                  pl.BlockSpec((D_MODEL, IN_COLS), lambda i: (0, 0))],
        out_specs=[pl.BlockSpec((tm, QKV_COLS), lambda i: (i, 0)),
                   pl.BlockSpec((tm, GATE_COLS), lambda i: (i, 0)),
                   pl.BlockSpec((tm, RWKV_SHIFT_COLS), lambda i: (i, 0))],
        out_shape=[jax.ShapeDtypeStruct((m, QKV_COLS), BF16),
                   jax.ShapeDtypeStruct((m, GATE_COLS), BF16),
                   jax.ShapeDtypeStruct((m, RWKV_SHIFT_COLS), F32)],
        compiler_params=pltpu.CompilerParams(dimension_semantics=("arbitrary",),
                                             vmem_limit_bytes=VMEM_LIMIT),
        name="in_proj",
    )(x2d, g.reshape(1, D_MODEL), w_bf16)


def _attn_kernel(q_ref, k_ref, v_ref, z_ref, bias_ref, lq_ref, g_ref, o_ref,
                 qz_ref, vt_ref, m_ref, acc_ref, *, tile, seq, lambda_init):
    i = pl.program_id(1)
    chains = 2 * ATT_HEADS
    dv = ATT_V_DIM

    @pl.when(i == 0)
    def _():
        def tr(c, carry):
            start = pl.multiple_of(c * tile, tile)
            for h in range(ATT_HEADS):
                blk = v_ref[0, pl.ds(start, tile), h * LANES:(h + 1) * LANES]
                vt_ref[h, 0:dv, pl.ds(start, tile)] = blk.astype(F32).T.astype(BF16)
                vt_ref[h, dv:dv + SUM_ROWS, pl.ds(start, tile)] = jnp.ones((SUM_ROWS, tile), BF16)
            return carry
        lax.fori_loop(0, seq // tile, tr, 0)

    q = q_ref[0]
    lane = lax.broadcasted_iota(jnp.int32, (tile, LANES), 1)
    for h in range(ATT_HEADS):
        qh = q[:, h * LANES:(h + 1) * LANES]
        qz_ref[2 * h] = jnp.where(lane < ATT_QK_DIM, qh, jnp.zeros_like(qh))
        qz_ref[2 * h + 1] = jnp.where(lane >= ATT_QK_DIM, qh, jnp.zeros_like(qh))

    m_ref[...] = jnp.full(m_ref.shape, NEG_INF, F32)
    acc_ref[...] = jnp.zeros(acc_ref.shape, F32)

    def update(jobs):
        steps = [(pl.multiple_of(j * tile, tile), bias_idx, c) for j, bias_idx in jobs for c in range(chains)]

        def scores(start, bias_idx, c):
            h = c // 2
            kb = k_ref[0, pl.ds(start, tile), h * LANES:(h + 1) * LANES]
            s = _dot_nt(kb, qz_ref[c])
            if bias_idx is not None:
                s = s + bias_ref[h, bias_idx]
            return s

        def softmax(start, c, s):
            m_prev = m_ref[c:c + 1, :]
            m_next = jnp.maximum(m_prev, jnp.max(s, axis=0, keepdims=True))
            alpha = jnp.exp2(m_prev - m_next)
            p = jnp.exp2(s - m_next)
            m_ref[c:c + 1, :] = m_next
            vtb = vt_ref[c // 2, :, pl.ds(start, tile)]
            return alpha, _dot(vtb, p.astype(BF16))

        def accumulate(c, alpha, pv):
            acc_ref[c] = alpha * acc_ref[c] + pv

        ahead = [scores(*st) for st in steps[:SCORE_LOOKAHEAD]]
        pending = None
        for n, (start, _, c) in enumerate(steps):
            s = ahead.pop(0)
            if n + SCORE_LOOKAHEAD < len(steps):
                ahead.append(scores(*steps[n + SCORE_LOOKAHEAD]))
            alpha, pv = softmax(start, c, s)
            if pending is not None:
                accumulate(*pending)
            pending = (c, alpha, pv)
        accumulate(*pending)

    n_far = jnp.maximum(i - 1, 0)
    fg = FAR_GROUP

    def far_body(trip, carry):
        update([(fg * trip + g, None) for g in range(fg)])
        return carry

    lax.fori_loop(0, n_far // fg, far_body, 0)

    @pl.when(i == 0)
    def _():
        update([(i, 0)])

    for rem in range(fg):
        @pl.when(jnp.logical_and(i >= 1, n_far % fg == rem))
        def _():
            update([(i - 1 - rem + g, None) for g in range(rem)] + [(i - 1, 1), (i, 0)])

    lq = lq_ref[...]
    lam = (jnp.exp(jnp.sum(lq[0:1] * lq[1:2], axis=1, keepdims=True))
           - jnp.exp(jnp.sum(lq[2:3] * lq[3:4], axis=1, keepdims=True)) + lambda_init)
    for h in range(ATT_HEADS):
        c = 2 * h
        ot = (acc_ref[c, 0:dv, :] / acc_ref[c, dv:dv + 1, :]
              - lam * (acc_ref[c + 1, 0:dv, :] / acc_ref[c + 1, dv:dv + 1, :]))
        o = ot.T
        ms = jnp.mean(o * o, axis=-1, keepdims=True)
        o = o * lax.rsqrt(ms + SUBLN_EPS) * g_ref[...] * (1.0 - lambda_init)
        z = z_ref[0, :, h * LANES:(h + 1) * LANES].astype(F32)
        o_ref[0, :, h * LANES:(h + 1) * LANES] = (o * (z * _sigmoid(z))).astype(o_ref.dtype)


def _attention(qkv, gates, bias_tiles, lam_qk_l, subln_g_l, lambda_init):
    b, s, _ = qkv.shape
    t = bias_tiles.shape[-1]
    kern = functools.partial(_attn_kernel, tile=t, seq=s, lambda_init=lambda_init)
    chains = 2 * ATT_HEADS
    return pl.pallas_call(
        kern,
        grid=(b, s // t),
        in_specs=[pl.BlockSpec((1, t, ATT_QK_COLS), lambda bi, i: (bi, i, 0)),
                  pl.BlockSpec((1, s, ATT_QK_COLS), lambda bi, i: (bi, 0, 1)),
                  pl.BlockSpec((1, s, ATT_WIDTH), lambda bi, i: (bi, 0, 2)),
                  pl.BlockSpec((1, t, ATT_WIDTH), lambda bi, i: (bi, i, 0)),
                  pl.BlockSpec((ATT_HEADS, 2, t, t), lambda bi, i: (0, 0, 0, 0)),
                  pl.BlockSpec((4, ATT_QK_DIM), lambda bi, i: (0, 0)),
                  pl.BlockSpec((1, ATT_V_DIM), lambda bi, i: (0, 0))],
        out_specs=pl.BlockSpec((1, t, ATT_WIDTH), lambda bi, i: (bi, i, 0)),
        out_shape=jax.ShapeDtypeStruct((b, s, ATT_WIDTH), BF16),
        scratch_shapes=[pltpu.VMEM((chains, t, LANES), BF16),
                        pltpu.VMEM((ATT_HEADS, ATT_V_DIM + SUM_ROWS, s), BF16),
                        pltpu.VMEM((chains, t), F32),
                        pltpu.VMEM((chains, ATT_V_DIM + SUM_ROWS, t), F32)],
        compiler_params=pltpu.CompilerParams(
            dimension_semantics=("arbitrary", "arbitrary"),
            vmem_limit_bytes=VMEM_LIMIT),
        name="diff_attention",
    )(qkv, qkv, qkv, gates, bias_tiles, lam_qk_l, subln_g_l.reshape(1, ATT_V_DIM))


def _segment_ones():
    idx = np.arange(RWKV_WIDTH) // RWKV_HEAD
    return jnp.asarray((idx[:, None] == idx[None, :]).astype(np.float32), dtype=BF16)


CHUNK = 64


def _dot_tn(a, b):
    return lax.dot_general(a, b, (((0,), (0,)), ((), ())), preferred_element_type=F32)


def _rwkv_kernel(p_ref, halo_ref, mu_ref, w0_ref, wup_ref, a0_ref, aup_ref, kkw_ref, ka_ref,
                 rk_ref, seg_ref, y_o, bonus_o, r_s, k_s, v_s, lw_s, a_s, b_s, state_ref,
                 m_s, j_s, rn_s, q2_s, *, tiles):
    n = pl.program_id(0)
    total = pl.num_programs(0) - 1
    cur = n % 2
    prev = 1 - cur
    i = n % tiles
    tm = p_ref.shape[1]
    c = CHUNK
    n_chunks = tm // c

    @pl.when(n == 0)
    def _():
        state_ref[...] = jnp.zeros(state_ref.shape, F32)
        m_s[1] = jnp.zeros(m_s.shape[1:], F32)
        j_s[1] = jnp.zeros(j_s.shape[1:], F32)
        rn_s[1] = jnp.zeros(rn_s.shape[1:], F32)
        q2_s[1] = jnp.zeros(q2_s.shape[1:], F32)

    def serial():
        starts_batch = (n - 1) % tiles == 0
        state = jnp.where(starts_batch, 0.0, state_ref[...])
        for g in range(n_chunks):
            rows = slice(g * c, (g + 1) * c)
            y_o[0, rows, :] = (_dot_f32(rn_s[prev, rows, :], state) + q2_s[prev, rows, :]).astype(y_o.dtype)
            state = _dot_f32(m_s[prev, g], state) + j_s[prev, g]
            yield
        state_ref[...] = state

    @pl.when(n == total)
    def _():
        _interleave(serial())

    @pl.when(n < total)
    def _():
        _interleave(serial(), _rwkv_local(i, p_ref, halo_ref, mu_ref, w0_ref, wup_ref, a0_ref, aup_ref, kkw_ref,
                                          ka_ref, rk_ref, seg_ref, bonus_o, r_s, k_s, v_s, lw_s, a_s, b_s,
                                          m_s, j_s, rn_s, q2_s, cur))


def _interleave(*gens):
    gens = list(gens)
    while gens:
        for gen in list(gens):
            try:
                next(gen)
            except StopIteration:
                gens.remove(gen)


def _rwkv_local(i, p_ref, halo_ref, mu_ref, w0_ref, wup_ref, a0_ref, aup_ref, kkw_ref, ka_ref, rk_ref,
                seg_ref, bonus_o, r_s, k_s, v_s, lw_s, a_s, b_s, m_s, j_s, rn_s, q2_s, cur):
    p = p_ref[0]
    tm = p.shape[0]
    last_prev = jnp.where(i > 0, halo_ref[0][SUBLANES - 1:SUBLANES, :], 0.0)
    row = lax.broadcasted_iota(jnp.int32, p.shape, 0)
    prev = jnp.where(row == 0, last_prev, pltpu.roll(p, 1, axis=0))
    p = p + (prev - p) * mu_ref[...]
    yield
    w3 = 3 * RWKV_WIDTH
    r = p[:, 0:RWKV_WIDTH]
    k = p[:, RWKV_WIDTH:2 * RWKV_WIDTH]
    v = p[:, 2 * RWKV_WIDTH:w3]
    wd = p[:, w3:w3 + DECAY_LORA]
    ad = p[:, w3 + DECAY_LORA:w3 + DECAY_LORA + ICLR_LORA]
    seg = seg_ref[...]

    x = -(w0_ref[...] + _dot_f32(jnp.tanh(wd), wup_ref[...]))
    softplus = jnp.maximum(x, 0.0) + jnp.log(1.0 + jnp.exp(-jnp.abs(x)))
    w = -softplus - 0.5
    yield
    a = _sigmoid(a0_ref[...] + _dot_f32(ad, aup_ref[...]))
    kk = k * kkw_ref[...]
    norm = jnp.sqrt(_seg_sum(kk * kk, seg))
    kk = kk / jnp.maximum(norm, 1e-12)
    k = k * (1.0 + (a - 1.0) * ka_ref[...])
    yield
    bonus_o[0] = (_seg_sum(r * k * rk_ref[...], seg) * v).astype(bonus_o.dtype)

    r_s[...] = r
    k_s[...] = k
    v_s[...] = v
    lw_s[...] = -jnp.exp(w)
    a_s[...] = -kk
    b_s[...] = kk * a
    yield

    c = CHUNK
    row_c = lax.broadcasted_iota(jnp.int32, (c, RWKV_WIDTH), 0)
    col_c = lax.broadcasted_iota(jnp.int32, (c, RWKV_WIDTH), 1) % c
    strict = col_c < row_c
    causal = col_c <= row_c
    eye = (col_c == row_c).astype(F32)
    r64 = lax.broadcasted_iota(jnp.int32, (c, c), 0)
    c64 = lax.broadcasted_iota(jnp.int32, (c, c), 1)
    tril_ones = (c64 <= r64).astype(BF16)
    rw_ = lax.broadcasted_iota(jnp.int32, (RWKV_WIDTH, RWKV_WIDTH), 0)
    cw_ = lax.broadcasted_iota(jnp.int32, (RWKV_WIDTH, RWKV_WIDTH), 1)
    block_diag = (rw_ // RWKV_HEAD) == (cw_ // RWKV_HEAD)
    diag = rw_ == cw_
    bf = lambda t: t.astype(BF16)
    by_head = lambda x: bf(jnp.where(block_diag, jnp.concatenate([x] * RWKV_HEADS, axis=0), 0.0))

    def transfers():
        gs = range(tm // c)
        rows = [slice(g * c, (g + 1) * c) for g in gs]
        rc, kc, vc, lw, ac, bc = ([ref[rows[g], :] for g in gs] for ref in (r_s, k_s, v_s, lw_s, a_s, b_s))
        cum = [_seg_sum_left(tril_ones, lw[g]) for g in gs]
        gam = [jnp.exp(cum[g]) for g in gs]
        gam_inv = [jnp.exp(-cum[g]) for g in gs]
        at = [ac[g] * jnp.exp(cum[g] - lw[g]) for g in gs]
        rt = [rc[g] * gam[g] for g in gs]
        bt = [bc[g] * gam_inv[g] for g in gs]
        kt = [kc[g] * gam_inv[g] for g in gs]
        g_end = [gam[g][c - 1:c, :] for g in gs]
        yield
        left = [bf(jnp.concatenate([at[g], rt[g]], axis=0)) for g in gs]
        prod_b = [_dot_nt(left[g], by_head(bt[g])) for g in gs]
        prod_k = [_dot_nt(left[g], by_head(kt[g])) for g in gs]
        yield
        npow = [jnp.where(strict, prod_b[g][0:c], 0.0) for g in gs]
        ntr = [jnp.where(strict, prod_k[g][0:c], 0.0) for g in gs]
        lb = [jnp.where(causal, prod_b[g][c:2 * c], 0.0) for g in gs]
        lk = [jnp.where(causal, prod_k[g][c:2 * c], 0.0) for g in gs]
        t_inv = [eye + npow[g] for g in gs]
        npow = [_dot(bf(npow[g]), by_head(npow[g])) for g in gs]
        yield
        for level in range(1, 6):
            lhs = [jnp.concatenate([t_inv[g], npow[g]], axis=0) if level < 5 else t_inv[g] for g in gs]
            both = [_dot(bf(lhs[g]), by_head(npow[g])) for g in gs]
            t_inv = [t_inv[g] + both[g][0:c] for g in gs]
            if level < 5:
                npow = [both[g][c:2 * c] for g in gs]
            yield
        by_v = [_dot(bf(jnp.concatenate([ntr[g], lk[g]], axis=0)), by_head(vc[g])) for g in gs]
        yield
        a_new = [_dot(bf(t_inv[g]), by_head(at[g])) for g in gs]
        q1 = [_dot(bf(t_inv[g]), by_head(by_v[g][0:c])) for g in gs]
        yield
        r_new = [rt[g] + _dot(bf(lb[g]), by_head(a_new[g])) for g in gs]
        q2 = [_dot(bf(lb[g]), by_head(q1[g])) + by_v[g][c:2 * c] for g in gs]
        yield
        m_bd = [jnp.where(block_diag, _dot_tn(bf(bt[g] * g_end[g]), bf(a_new[g])), 0.0)
                + jnp.where(diag, g_end[g], 0.0) for g in gs]
        j_bd = [jnp.where(block_diag,
                          _dot_tn(bf(jnp.concatenate([bt[g] * g_end[g], kt[g] * g_end[g]], axis=0)),
                                  bf(jnp.concatenate([q1[g], vc[g]], axis=0))), 0.0) for g in gs]

        for g in gs:
            m_s[cur, g] = m_bd[g]
            j_s[cur, g] = j_bd[g]
            rn_s[cur, rows[g], :] = r_new[g]
            q2_s[cur, rows[g], :] = q2[g]

    yield from transfers()


def _seg_sum_left(ones_bf16, x):
    hi, lo = _split2(x)
    return _dot(ones_bf16, hi) + _dot(ones_bf16, lo)


def _rwkv_mix(rw, mu, w0, w_up, a0, a_up, k_k, k_a, r_k):
    b, s, _ = rw.shape
    tm = min(TOK_TILE, s)
    tiles = s // tm
    total = b * tiles
    row = lambda a, n: a.reshape(1, n).astype(F32)
    full = lambda shape: pl.BlockSpec(shape, lambda n: (0,) * len(shape))
    halo_blocks = tm // SUBLANES
    cur_tile = lambda n: jnp.minimum(n, total - 1)
    prev_tile = lambda n: jnp.maximum(n - 1, 0)
    out_sds = jax.ShapeDtypeStruct((b, s, RWKV_WIDTH), BF16)
    tile_scratch = pltpu.VMEM((tm, RWKV_WIDTH), F32)
    n_chunks = tm // CHUNK
    transfer_scratch = pltpu.VMEM((2, n_chunks, RWKV_WIDTH, RWKV_WIDTH), F32)
    return pl.pallas_call(
        functools.partial(_rwkv_kernel, tiles=tiles),
        grid=(total + 1,),
        in_specs=[pl.BlockSpec((1, tm, RWKV_SHIFT_COLS), lambda n: (cur_tile(n) // tiles, cur_tile(n) % tiles, 0)),
                  pl.BlockSpec((1, SUBLANES, RWKV_SHIFT_COLS),
                               lambda n: (cur_tile(n) // tiles,
                                          jnp.maximum((cur_tile(n) % tiles) * halo_blocks - 1, 0), 0)),
                  full((1, RWKV_SHIFT_COLS)), full((1, RWKV_WIDTH)), full((DECAY_LORA, RWKV_WIDTH)),
                  full((1, RWKV_WIDTH)), full((ICLR_LORA, RWKV_WIDTH)), full((1, RWKV_WIDTH)),
                  full((1, RWKV_WIDTH)), full((1, RWKV_WIDTH)), full((RWKV_WIDTH, RWKV_WIDTH))],
        out_specs=[pl.BlockSpec((1, tm, RWKV_WIDTH), lambda n: (prev_tile(n) // tiles, prev_tile(n) % tiles, 0)),
                   pl.BlockSpec((1, tm, RWKV_WIDTH), lambda n: (cur_tile(n) // tiles, cur_tile(n) % tiles, 0))],
        out_shape=[out_sds] * 2,
        scratch_shapes=[tile_scratch] * 6 + [pltpu.VMEM((RWKV_WIDTH, RWKV_WIDTH), F32),
                                             transfer_scratch, transfer_scratch,
                                             pltpu.VMEM((2, tm, RWKV_WIDTH), F32),
                                             pltpu.VMEM((2, tm, RWKV_WIDTH), F32)],
        compiler_params=pltpu.CompilerParams(dimension_semantics=("arbitrary",),
                                             vmem_limit_bytes=VMEM_LIMIT),
        name="rwkv_mix",
    )(rw, rw, row(mu, RWKV_SHIFT_COLS), row(w0, RWKV_WIDTH), w_up.astype(F32), row(a0, RWKV_WIDTH),
      a_up.astype(F32), row(k_k, RWKV_WIDTH), row(k_a, RWKV_WIDTH), row(r_k, RWKV_WIDTH), _segment_ones())


def _shift_rows(u, halo_u, shift):
    rolled = pltpu.roll(u, shift, axis=0)
    head = jnp.where(lax.broadcasted_iota(jnp.int32, halo_u.shape, 0) < shift,
                     pltpu.roll(halo_u, shift, axis=0), rolled[0:SUBLANES])
    return jnp.concatenate([head, rolled[SUBLANES:]], axis=0)


def _out_kernel(x_ref, att_ref, gate_ref, halo_ref, y_ref, bonus_ref, cw_ref, lng_ref, lnb_ref, seg_ref,
                wout_ref, fg_ref, o_ref, *, final):
    i = pl.program_id(1)
    c0 = ATT_WIDTH
    cols = lambda ref, rows, j, width=CONV_WIDTH: ref[0, rows, c0 + j * CONV_WIDTH:c0 + j * CONV_WIDTH + width].astype(F32)
    every = slice(None)
    cb, cc, ch, zc = (cols(gate_ref, every, j) for j in range(4))
    zr = cols(gate_ref, every, 4, RWKV_WIDTH)
    u = cc * ch
    halo_u = (cols(halo_ref, every, 1) * cols(halo_ref, every, 2))[HALO_ROWS - SUBLANES:HALO_ROWS]
    halo_u = jnp.where(i > 0, halo_u, 0.0)
    cw = cw_ref[...]
    conv = (cw[0:1] * _shift_rows(u, halo_u, 2) + cw[1:2] * _shift_rows(u, halo_u, 1)) + cw[2:3] * u
    cv = (cb * conv) * (zc * _sigmoid(zc))

    seg = seg_ref[...]
    y = y_ref[0].astype(F32)
    inv_n = 1.0 / RWKV_HEAD
    mean = _seg_sum(y, seg) * inv_n
    d = y - mean
    var = _seg_sum(d * d, seg) * inv_n
    yn = d * lax.rsqrt(var + GN_EPS) * lng_ref[...] + lnb_ref[...]
    rw = (yn + bonus_ref[0].astype(F32)) * (zr * _sigmoid(zr))

    upd = (_dot(att_ref[0], wout_ref[0:ATT_WIDTH, :])
           + _dot(cv.astype(BF16), wout_ref[ATT_WIDTH:ATT_WIDTH + CONV_WIDTH, :])
           + _dot(rw.astype(BF16), wout_ref[ATT_WIDTH + CONV_WIDTH:, :]))
    xn = x_ref[0] + upd
    if final:
        ms = jnp.mean(xn * xn, axis=-1, keepdims=True)
        xn = xn * lax.rsqrt(ms + NORM_EPS) * fg_ref[...]
    o_ref[0] = xn


def _out_proj(x, att, gates, y, bonus, conv_w_l, lnx_g_l, lnx_b_l, w_out_bf16, final_g, final):
    b, s, _ = x.shape
    tm = min(TOK_TILE, s)
    row = lambda a, n: a.reshape(1, n).astype(F32)
    full = lambda shape: pl.BlockSpec(shape, lambda bi, i: (0,) * len(shape))
    tok = lambda n: pl.BlockSpec((1, tm, n), lambda bi, i: (bi, i, 0))
    halo_blocks = tm // HALO_ROWS
    return pl.pallas_call(
        functools.partial(_out_kernel, final=final),
        grid=(b, s // tm),
        in_specs=[tok(D_MODEL), tok(ATT_WIDTH), tok(GATE_COLS),
                  pl.BlockSpec((1, HALO_ROWS, GATE_COLS),
                               lambda bi, i: (bi, jnp.maximum(i * halo_blocks - 1, 0), 0)),
                  tok(RWKV_WIDTH), tok(RWKV_WIDTH),
                  full((CONV_K, CONV_WIDTH)), full((1, RWKV_WIDTH)), full((1, RWKV_WIDTH)),
                  full((RWKV_WIDTH, RWKV_WIDTH)), full((D_MODEL, D_MODEL)), full((1, D_MODEL))],
        out_specs=tok(D_MODEL),
        out_shape=jax.ShapeDtypeStruct((b, s, D_MODEL), F32),
        compiler_params=pltpu.CompilerParams(dimension_semantics=("arbitrary", "arbitrary"),
                                             vmem_limit_bytes=VMEM_LIMIT),
        name="out_proj",
    )(x, att, gates, gates, y, bonus, conv_w_l.astype(F32),
      row(lnx_g_l, RWKV_WIDTH), row(lnx_b_l, RWKV_WIDTH), _segment_ones(), w_out_bf16,
      row(final_g, D_MODEL))


def kernel(x, norm_g, w_in, w_out, final_norm_g, rel_bias, lam_qk, subln_g, conv_w, rwkv_mu, w0, w_up,
           a0, a_up, k_k, k_a, r_k, lnx_g, lnx_b):
    b, s, _ = x.shape
    tile = min(ATT_TILE, s)
    bias = _bias_tiles(rel_bias, tile)
    w_in_b = w_in.astype(BF16)
    w_out_b = w_out.astype(BF16)
    x = x.astype(F32)
    for l in range(DEPTH):
        lambda_init = 0.8 - 0.6 * math.exp(-0.3 * l)
        qkv, gates, rw = _in_proj(x.reshape(b * s, D_MODEL), norm_g[l].astype(F32), w_in_b[l])
        qkv = qkv.reshape(b, s, QKV_COLS)
        gates = gates.reshape(b, s, GATE_COLS)
        rw = rw.reshape(b, s, RWKV_SHIFT_COLS)
        att = _attention(qkv, gates, bias, lam_qk[l].astype(F32), subln_g[l].astype(F32), lambda_init)
        y, bonus = _rwkv_mix(rw, rwkv_mu[l], w0[l], w_up[l], a0[l], a_up[l], k_k[l], k_a[l], r_k[l])
        x = _out_proj(x, att, gates, y, bonus, conv_w[l], lnx_g[l], lnx_b[l], w_out_b[l],
                      final_norm_g, final=(l == DEPTH - 1))
    return x
```

```python
import functools
import math

import numpy as np
import jax
import jax.numpy as jnp
from jax import lax
from jax.experimental import pallas as pl
from jax.experimental.pallas import tpu as pltpu

F32 = jnp.float32
BF16 = jnp.bfloat16

D_MODEL = 1024
DEPTH = 4
ATT_HEADS = 4
ATT_QK_DIM = 64
ATT_V_DIM = 128
ATT_WIDTH = ATT_HEADS * ATT_V_DIM
ATT_QK_COLS = ATT_HEADS * 2 * ATT_QK_DIM
CONV_WIDTH = 256
CONV_K = 3
RWKV_HEAD = 64
RWKV_WIDTH = 256
RWKV_HEADS = 4
DECAY_LORA = 64
ICLR_LORA = 64
RWKV_SHIFT_COLS = 3 * RWKV_WIDTH + DECAY_LORA + ICLR_LORA
IN_COLS = 2 * ATT_QK_COLS + 2 * ATT_WIDTH + 4 * CONV_WIDTH + RWKV_SHIFT_COLS + RWKV_WIDTH
N_BUCKETS = 32
MAX_DISTANCE = 128
NEG_INF = -1e30
NORM_EPS = 1e-6
SUBLN_EPS = 1e-5
GN_EPS = 64e-5

QKV_COLS = 2 * ATT_QK_COLS + ATT_WIDTH
GATE_COLS = ATT_WIDTH + 4 * CONV_WIDTH + RWKV_WIDTH
_OFF_ZATT = QKV_COLS
_OFF_RW = _OFF_ZATT + ATT_WIDTH + 4 * CONV_WIDTH
_OFF_ZRW = _OFF_RW + RWKV_SHIFT_COLS

LOG2E = math.log2(math.e)
Q_SCALE = ATT_QK_DIM ** -0.5 * LOG2E

LANES = 128
SUBLANES = 8
HALO_ROWS = 16
VMEM_LIMIT = 56 * 1024 * 1024

ATT_TILE = 256
SCORE_LOOKAHEAD = 6
FAR_GROUP = 4
SUM_ROWS = 16
TOK_TILE = 512
OUT_TILE = 1024


def _dot(a, b):
    return jnp.dot(a, b, preferred_element_type=F32)


def _dot_nt(a, b):
    return lax.dot_general(a, b, (((1,), (1,)), ((), ())), preferred_element_type=F32)


def _split2(x):
    hi = x.astype(BF16)
    lo = (x - hi.astype(F32)).astype(BF16)
    return hi, lo


def _dot_f32(a, b):
    ah, al = _split2(a)
    bh, bl = _split2(b)
    return _dot(ah, bh) + (_dot(ah, bl) + _dot(al, bh))


def _seg_sum(x, seg):
    hi, lo = _split2(x)
    return _dot(hi, seg) + _dot(lo, seg)


def _sigmoid(x):
    return 1.0 / (1.0 + jnp.exp(-x))


def _bucket_tiles(tile):
    key = np.arange(tile)[:, None]
    query = np.arange(tile)[None, :]
    out = []
    for off in (0, tile):
        dist = query - key + off
        n = np.maximum(dist, 0)
        max_exact = N_BUCKETS // 2
        nf = np.maximum(n, 1).astype(np.float32)
        large = max_exact + (np.log(nf / np.float32(max_exact)) / np.float32(math.log(MAX_DISTANCE / max_exact))
                             * np.float32(N_BUCKETS - max_exact)).astype(np.int32)
        large = np.minimum(large, N_BUCKETS - 1)
        bucket = np.where(n < max_exact, n, large)
        out.append(np.where(dist >= 0, bucket, -1))
    return np.stack(out).astype(np.int32)


def _bias_kernel(tab_ref, bucket_ref, o_ref):
    h = pl.program_id(0)
    bk = bucket_ref[0]
    far = tab_ref[N_BUCKETS - 1, h]
    val = jnp.zeros(bk.shape, F32)
    for i in range(N_BUCKETS - 1):
        val = jnp.where(bk == i, (tab_ref[i, h] - far) * LOG2E, val)
    o_ref[0, 0] = jnp.where(bk < 0, NEG_INF, val)


def _bias_tiles(rel_bias, tile):
    buckets = jnp.asarray(_bucket_tiles(tile))
    return pl.pallas_call(
        _bias_kernel,
        grid=(ATT_HEADS, 2),
        in_specs=[pl.BlockSpec(memory_space=pltpu.SMEM),
                  pl.BlockSpec((1, tile, tile), lambda h, d: (d, 0, 0))],
        out_specs=pl.BlockSpec((1, 1, tile, tile), lambda h, d: (h, d, 0, 0)),
        out_shape=jax.ShapeDtypeStruct((ATT_HEADS, 2, tile, tile), F32),
        name="bias_tiles",
    )(rel_bias.astype(F32), buckets)


def _col_chunks(start, stop, width=512):
    c = start
    while c < stop:
        yield c, min(width, stop - c)
        c += width


def _in_proj_kernel(x_ref, g_ref, w_ref, qkv_ref, gate_ref, rw_ref):
    x = x_ref[...]
    ms = jnp.mean(x * x, axis=-1, keepdims=True)
    h = (x * lax.rsqrt(ms + NORM_EPS) * g_ref[...]).astype(BF16)
    for c, n in _col_chunks(0, QKV_COLS, ATT_QK_COLS):
        acc = _dot(h, w_ref[:, c:c + n])
        if c < ATT_QK_COLS:
            acc = acc * Q_SCALE
        qkv_ref[:, c:c + n] = acc.astype(BF16)
    for c, n in _col_chunks(_OFF_ZATT, _OFF_RW):
        gate_ref[:, c - _OFF_ZATT:c - _OFF_ZATT + n] = _dot(h, w_ref[:, c:c + n]).astype(BF16)
    for c, n in _col_chunks(_OFF_RW, _OFF_ZRW):
        rw_ref[:, c - _OFF_RW:c - _OFF_RW + n] = _dot(h, w_ref[:, c:c + n])
    g0 = _OFF_RW - _OFF_ZATT
    gate_ref[:, g0:g0 + RWKV_WIDTH] = _dot(h, w_ref[:, _OFF_ZRW:IN_COLS]).astype(BF16)


def _in_proj(x2d, g, w_bf16):
    m = x2d.shape[0]
    tm = TOK_TILE
    return pl.pallas_call(
        _in_proj_kernel,
        grid=(m // tm,),
        in_specs=[pl.BlockSpec((tm, D_MODEL), lambda i: (i, 0)),
                  pl.BlockSpec((1, D_MODEL), lambda i: (0, 0)),
                  pl.BlockSpec((D_MODEL, IN_COLS), lambda i: (0, 0))],
        out_specs=[pl.BlockSpec((tm, QKV_COLS), lambda i: (i, 0)),
                   pl.BlockSpec((tm, GATE_COLS), lambda i: (i, 0)),
                   pl.BlockSpec((tm, RWKV_SHIFT_COLS), lambda i: (i, 0))],
        out_shape=[jax.ShapeDtypeStruct((m, QKV_COLS), BF16),
                   jax.ShapeDtypeStruct((m, GATE_COLS), BF16),
                   jax.ShapeDtypeStruct((m, RWKV_SHIFT_COLS), F32)],
        compiler_params=pltpu.CompilerParams(dimension_semantics=("arbitrary",),
                                             vmem_limit_bytes=VMEM_LIMIT),
        name="in_proj",
    )(x2d, g.reshape(1, D_MODEL), w_bf16)


def _attn_kernel(q_ref, k_ref, v_ref, z_ref, bias_ref, lq_ref, g_ref, o_ref,
                 qz_ref, vt_ref, m_ref, acc_ref, *, tile, seq, lambda_init):
    i = pl.program_id(1)
    chains = 2 * ATT_HEADS
    dv = ATT_V_DIM

    @pl.when(i == 0)
    def _():
        def tr(c, carry):
            start = pl.multiple_of(c * tile, tile)
            for h in range(ATT_HEADS):
                blk = v_ref[0, pl.ds(start, tile), h * LANES:(h + 1) * LANES]
                vt_ref[h, 0:dv, pl.ds(start, tile)] = blk.astype(F32).T.astype(BF16)
                vt_ref[h, dv:dv + SUM_ROWS, pl.ds(start, tile)] = jnp.ones((SUM_ROWS, tile), BF16)
            return carry
        lax.fori_loop(0, seq // tile, tr, 0)

    q = q_ref[0]
    lane = lax.broadcasted_iota(jnp.int32, (tile, LANES), 1)
    for h in range(ATT_HEADS):
        qh = q[:, h * LANES:(h + 1) * LANES]
        transposed = lambda t: t.astype(F32).T.astype(BF16)
        qz_ref[2 * h] = transposed(jnp.where(lane < ATT_QK_DIM, qh, jnp.zeros_like(qh)))
        qz_ref[2 * h + 1] = transposed(jnp.where(lane >= ATT_QK_DIM, qh, jnp.zeros_like(qh)))

    m_ref[...] = jnp.full(m_ref.shape, NEG_INF, F32)
    acc_ref[...] = jnp.zeros(acc_ref.shape, F32)

    def update(jobs):
        steps = [(pl.multiple_of(j * tile, tile), bias_idx, c) for j, bias_idx in jobs for c in range(chains)]

        def scores(start, bias_idx, c):
            h = c // 2
            kb = k_ref[0, pl.ds(start, tile), h * LANES:(h + 1) * LANES]
            s = _dot(kb, qz_ref[c])
            if bias_idx is not None:
                s = s + bias_ref[h, bias_idx]
            return s

        def softmax(start, c, s):
            m_prev = m_ref[c:c + 1, :]
            m_next = jnp.maximum(m_prev, jnp.max(s, axis=0, keepdims=True))
            alpha = jnp.exp2(m_prev - m_next)
            p = jnp.exp2(s - m_next)
            m_ref[c:c + 1, :] = m_next
            vtb = vt_ref[c // 2, :, pl.ds(start, tile)]
            return alpha, _dot(vtb, p.astype(BF16))

        def accumulate(c, alpha, pv):
            acc_ref[c] = alpha * acc_ref[c] + pv

        ahead = [scores(*st) for st in steps[:SCORE_LOOKAHEAD]]
        pending = None
        for n, (start, _, c) in enumerate(steps):
            s = ahead.pop(0)
            if n + SCORE_LOOKAHEAD < len(steps):
                ahead.append(scores(*steps[n + SCORE_LOOKAHEAD]))
            alpha, pv = softmax(start, c, s)
            if pending is not None:
                accumulate(*pending)
            pending = (c, alpha, pv)
        accumulate(*pending)

    n_far = jnp.maximum(i - 1, 0)
    fg = FAR_GROUP

    def far_body(trip, carry):
        update([(fg * trip + g, None) for g in range(fg)])
        return carry

    lax.fori_loop(0, n_far // fg, far_body, 0)

    @pl.when(i == 0)
    def _():
        update([(i, 0)])

    for rem in range(fg):
        @pl.when(jnp.logical_and(i >= 1, n_far % fg == rem))
        def _():
            update([(i - 1 - rem + g, None) for g in range(rem)] + [(i - 1, 1), (i, 0)])

    lq = lq_ref[...]
    lam = (jnp.exp(jnp.sum(lq[0:1] * lq[1:2], axis=1, keepdims=True))
           - jnp.exp(jnp.sum(lq[2:3] * lq[3:4], axis=1, keepdims=True)) + lambda_init)
    for h in range(ATT_HEADS):
        c = 2 * h
        ot = (acc_ref[c, 0:dv, :] / acc_ref[c, dv:dv + 1, :]
              - lam * (acc_ref[c + 1, 0:dv, :] / acc_ref[c + 1, dv:dv + 1, :]))
        o = ot.T
        ms = jnp.mean(o * o, axis=-1, keepdims=True)
        o = o * lax.rsqrt(ms + SUBLN_EPS) * g_ref[...] * (1.0 - lambda_init)
        z = z_ref[0, :, h * LANES:(h + 1) * LANES].astype(F32)
        o_ref[0, :, h * LANES:(h + 1) * LANES] = (o * (z * _sigmoid(z))).astype(o_ref.dtype)


def _attention(qkv, gates, bias_tiles, lam_qk_l, subln_g_l, lambda_init):
    b, s, _ = qkv.shape
    t = bias_tiles.shape[-1]
    kern = functools.partial(_attn_kernel, tile=t, seq=s, lambda_init=lambda_init)
    chains = 2 * ATT_HEADS
    return pl.pallas_call(
        kern,
        grid=(b, s // t),
        in_specs=[pl.BlockSpec((1, t, ATT_QK_COLS), lambda bi, i: (bi, i, 0)),
                  pl.BlockSpec((1, s, ATT_QK_COLS), lambda bi, i: (bi, 0, 1)),
                  pl.BlockSpec((1, s, ATT_WIDTH), lambda bi, i: (bi, 0, 2)),
                  pl.BlockSpec((1, t, ATT_WIDTH), lambda bi, i: (bi, i, 0)),
                  pl.BlockSpec((ATT_HEADS, 2, t, t), lambda bi, i: (0, 0, 0, 0)),
                  pl.BlockSpec((4, ATT_QK_DIM), lambda bi, i: (0, 0)),
                  pl.BlockSpec((1, ATT_V_DIM), lambda bi, i: (0, 0))],
        out_specs=pl.BlockSpec((1, t, ATT_WIDTH), lambda bi, i: (bi, i, 0)),
        out_shape=jax.ShapeDtypeStruct((b, s, ATT_WIDTH), BF16),
        scratch_shapes=[pltpu.VMEM((chains, LANES, t), BF16),
                        pltpu.VMEM((ATT_HEADS, ATT_V_DIM + SUM_ROWS, s), BF16),
                        pltpu.VMEM((chains, t), F32),
                        pltpu.VMEM((chains, ATT_V_DIM + SUM_ROWS, t), F32)],
        compiler_params=pltpu.CompilerParams(
            dimension_semantics=("arbitrary", "arbitrary"),
            vmem_limit_bytes=VMEM_LIMIT),
        name="diff_attention",
    )(qkv, qkv, qkv, gates, bias_tiles, lam_qk_l, subln_g_l.reshape(1, ATT_V_DIM))


def _segment_ones():
    idx = np.arange(RWKV_WIDTH) // RWKV_HEAD
    return jnp.asarray((idx[:, None] == idx[None, :]).astype(np.float32), dtype=BF16)


CHUNK = 64


def _dot_tn(a, b):
    return lax.dot_general(a, b, (((0,), (0,)), ((), ())), preferred_element_type=F32)


def _rwkv_kernel(p_ref, halo_ref, mu_ref, w0_ref, wup_ref, a0_ref, aup_ref, kkw_ref, ka_ref,
                 rk_ref, seg_ref, y_o, bonus_o, r_s, k_s, v_s, lw_s, a_s, b_s, state_ref,
                 m_s, j_s, rn_s, q2_s, *, tiles):
    n = pl.program_id(0)
    total = pl.num_programs(0) - 1
    cur = n % 2
    prev = 1 - cur
    i = n % tiles
    tm = p_ref.shape[1]
    c = CHUNK
    n_chunks = tm // c

    @pl.when(n == 0)
    def _():
        state_ref[...] = jnp.zeros(state_ref.shape, F32)
        m_s[1] = jnp.zeros(m_s.shape[1:], F32)
        j_s[1] = jnp.zeros(j_s.shape[1:], F32)
        rn_s[1] = jnp.zeros(rn_s.shape[1:], F32)
        q2_s[1] = jnp.zeros(q2_s.shape[1:], F32)

    def serial():
        starts_batch = (n - 1) % tiles == 0
        state = jnp.where(starts_batch, 0.0, state_ref[...])
        for g in range(n_chunks):
            rows = slice(g * c, (g + 1) * c)
            y_o[0, rows, :] = (_dot_f32(rn_s[prev, rows, :], state) + q2_s[prev, rows, :]).astype(y_o.dtype)
            state = _dot_f32(m_s[prev, g], state) + j_s[prev, g]
            yield
        state_ref[...] = state

    @pl.when(n == total)
    def _():
        _interleave(serial())

    @pl.when(n < total)
    def _():
        _interleave(serial(), _rwkv_local(i, p_ref, halo_ref, mu_ref, w0_ref, wup_ref, a0_ref, aup_ref, kkw_ref,
                                          ka_ref, rk_ref, seg_ref, bonus_o, r_s, k_s, v_s, lw_s, a_s, b_s,
                                          m_s, j_s, rn_s, q2_s, cur))


def _interleave(*gens):
    gens = list(gens)
    while gens:
        for gen in list(gens):
            try:
                next(gen)
            except StopIteration:
                gens.remove(gen)


def _rwkv_local(i, p_ref, halo_ref, mu_ref, w0_ref, wup_ref, a0_ref, aup_ref, kkw_ref, ka_ref, rk_ref,
                seg_ref, bonus_o, r_s, k_s, v_s, lw_s, a_s, b_s, m_s, j_s, rn_s, q2_s, cur):
    p = p_ref[0]
    tm = p.shape[0]
    last_prev = jnp.where(i > 0, halo_ref[0][SUBLANES - 1:SUBLANES, :], 0.0)
    row = lax.broadcasted_iota(jnp.int32, p.shape, 0)
    prev = jnp.where(row == 0, last_prev, pltpu.roll(p, 1, axis=0))
    p = p + (prev - p) * mu_ref[...]
    yield
    w3 = 3 * RWKV_WIDTH
    r = p[:, 0:RWKV_WIDTH]
    k = p[:, RWKV_WIDTH:2 * RWKV_WIDTH]
    v = p[:, 2 * RWKV_WIDTH:w3]
    wd = p[:, w3:w3 + DECAY_LORA]
    ad = p[:, w3 + DECAY_LORA:w3 + DECAY_LORA + ICLR_LORA]
    seg = seg_ref[...]

    x = -(w0_ref[...] + _dot_f32(jnp.tanh(wd), wup_ref[...]))
    softplus = jnp.maximum(x, 0.0) + jnp.log(1.0 + jnp.exp(-jnp.abs(x)))
    w = -softplus - 0.5
    yield
    a = _sigmoid(a0_ref[...] + _dot_f32(ad, aup_ref[...]))
    kk = k * kkw_ref[...]
    norm = jnp.sqrt(_seg_sum(kk * kk, seg))
    kk = kk / jnp.maximum(norm, 1e-12)
    k = k * (1.0 + (a - 1.0) * ka_ref[...])
    yield
    bonus_o[0] = (_seg_sum(r * k * rk_ref[...], seg) * v).astype(bonus_o.dtype)

    r_s[...] = r
    k_s[...] = k
    v_s[...] = v
    lw_s[...] = -jnp.exp(w)
    a_s[...] = -kk
    b_s[...] = kk * a
    yield

    c = CHUNK
    row_c = lax.broadcasted_iota(jnp.int32, (c, RWKV_WIDTH), 0)
    col_c = lax.broadcasted_iota(jnp.int32, (c, RWKV_WIDTH), 1) % c
    strict = col_c < row_c
    causal = col_c <= row_c
    eye = (col_c == row_c).astype(F32)
    r64 = lax.broadcasted_iota(jnp.int32, (c, c), 0)
    c64 = lax.broadcasted_iota(jnp.int32, (c, c), 1)
    tril_ones = (c64 <= r64).astype(BF16)
    rw_ = lax.broadcasted_iota(jnp.int32, (RWKV_WIDTH, RWKV_WIDTH), 0)
    cw_ = lax.broadcasted_iota(jnp.int32, (RWKV_WIDTH, RWKV_WIDTH), 1)
    block_diag = (rw_ // RWKV_HEAD) == (cw_ // RWKV_HEAD)
    diag = rw_ == cw_
    bf = lambda t: t.astype(BF16)
    by_head = lambda x: bf(jnp.where(block_diag, jnp.concatenate([x] * RWKV_HEADS, axis=0), 0.0))

    def transfers():
        gs = range(tm // c)
        rows = [slice(g * c, (g + 1) * c) for g in gs]
        rc, kc, vc, lw, ac, bc = ([ref[rows[g], :] for g in gs] for ref in (r_s, k_s, v_s, lw_s, a_s, b_s))
        cum = [_seg_sum_left(tril_ones, lw[g]) for g in gs]
        gam = [jnp.exp(cum[g]) for g in gs]
        gam_inv = [jnp.exp(-cum[g]) for g in gs]
        at = [ac[g] * jnp.exp(cum[g] - lw[g]) for g in gs]
        rt = [rc[g] * gam[g] for g in gs]
        bt = [bc[g] * gam_inv[g] for g in gs]
        kt = [kc[g] * gam_inv[g] for g in gs]
        g_end = [gam[g][c - 1:c, :] for g in gs]
        yield
        left = [bf(jnp.concatenate([at[g], rt[g]], axis=0)) for g in gs]
        prod_b = [_dot_nt(left[g], by_head(bt[g])) for g in gs]
        prod_k = [_dot_nt(left[g], by_head(kt[g])) for g in gs]
        yield
        npow = [jnp.where(strict, prod_b[g][0:c], 0.0) for g in gs]
        ntr = [jnp.where(strict, prod_k[g][0:c], 0.0) for g in gs]
        lb = [jnp.where(causal, prod_b[g][c:2 * c], 0.0) for g in gs]
        lk = [jnp.where(causal, prod_k[g][c:2 * c], 0.0) for g in gs]
        t_inv = [eye + npow[g] for g in gs]
        npow = [_dot(bf(npow[g]), by_head(npow[g])) for g in gs]
        yield
        for level in range(1, 6):
            lhs = [jnp.concatenate([t_inv[g], npow[g]], axis=0) if level < 5 else t_inv[g] for g in gs]
            both = [_dot(bf(lhs[g]), by_head(npow[g])) for g in gs]
            t_inv = [t_inv[g] + both[g][0:c] for g in gs]
            if level < 5:
                npow = [both[g][c:2 * c] for g in gs]
            yield
        by_v = [_dot(bf(jnp.concatenate([ntr[g], lk[g]], axis=0)), by_head(vc[g])) for g in gs]
        yield
        a_new = [_dot(bf(t_inv[g]), by_head(at[g])) for g in gs]
        q1 = [_dot(bf(t_inv[g]), by_head(by_v[g][0:c])) for g in gs]
        yield
        r_new = [rt[g] + _dot(bf(lb[g]), by_head(a_new[g])) for g in gs]
        q2 = [_dot(bf(lb[g]), by_head(q1[g])) + by_v[g][c:2 * c] for g in gs]
        yield
        m_bd = [jnp.where(block_diag, _dot_tn(bf(bt[g] * g_end[g]), bf(a_new[g])), 0.0)
                + jnp.where(diag, g_end[g], 0.0) for g in gs]
        j_bd = [jnp.where(block_diag,
                          _dot_tn(bf(jnp.concatenate([bt[g] * g_end[g], kt[g] * g_end[g]], axis=0)),
                                  bf(jnp.concatenate([q1[g], vc[g]], axis=0))), 0.0) for g in gs]

        for g in gs:
            m_s[cur, g] = m_bd[g]
            j_s[cur, g] = j_bd[g]
            rn_s[cur, rows[g], :] = r_new[g]
            q2_s[cur, rows[g], :] = q2[g]

    yield from transfers()


def _seg_sum_left(ones_bf16, x):
    hi, lo = _split2(x)
    return _dot(ones_bf16, hi) + _dot(ones_bf16, lo)


def _rwkv_mix(rw, mu, w0, w_up, a0, a_up, k_k, k_a, r_k):
    b, s, _ = rw.shape
    tm = min(TOK_TILE, s)
    tiles = s // tm
    total = b * tiles
    row = lambda a, n: a.reshape(1, n).astype(F32)
    full = lambda shape: pl.BlockSpec(shape, lambda n: (0,) * len(shape))
    halo_blocks = tm // SUBLANES
    cur_tile = lambda n: jnp.minimum(n, total - 1)
    prev_tile = lambda n: jnp.maximum(n - 1, 0)
    out_sds = jax.ShapeDtypeStruct((b, s, RWKV_WIDTH), BF16)
    tile_scratch = pltpu.VMEM((tm, RWKV_WIDTH), F32)
    n_chunks = tm // CHUNK
    transfer_scratch = pltpu.VMEM((2, n_chunks, RWKV_WIDTH, RWKV_WIDTH), F32)
    return pl.pallas_call(
        functools.partial(_rwkv_kernel, tiles=tiles),
        grid=(total + 1,),
        in_specs=[pl.BlockSpec((1, tm, RWKV_SHIFT_COLS), lambda n: (cur_tile(n) // tiles, cur_tile(n) % tiles, 0)),
                  pl.BlockSpec((1, SUBLANES, RWKV_SHIFT_COLS),
                               lambda n: (cur_tile(n) // tiles,
                                          jnp.maximum((cur_tile(n) % tiles) * halo_blocks - 1, 0), 0)),
                  full((1, RWKV_SHIFT_COLS)), full((1, RWKV_WIDTH)), full((DECAY_LORA, RWKV_WIDTH)),
                  full((1, RWKV_WIDTH)), full((ICLR_LORA, RWKV_WIDTH)), full((1, RWKV_WIDTH)),
                  full((1, RWKV_WIDTH)), full((1, RWKV_WIDTH)), full((RWKV_WIDTH, RWKV_WIDTH))],
        out_specs=[pl.BlockSpec((1, tm, RWKV_WIDTH), lambda n: (prev_tile(n) // tiles, prev_tile(n) % tiles, 0)),
                   pl.BlockSpec((1, tm, RWKV_WIDTH), lambda n: (cur_tile(n) // tiles, cur_tile(n) % tiles, 0))],
        out_shape=[out_sds] * 2,
        scratch_shapes=[tile_scratch] * 6 + [pltpu.VMEM((RWKV_WIDTH, RWKV_WIDTH), F32),
                                             transfer_scratch, transfer_scratch,
                                             pltpu.VMEM((2, tm, RWKV_WIDTH), F32),
                                             pltpu.VMEM((2, tm, RWKV_WIDTH), F32)],
        compiler_params=pltpu.CompilerParams(dimension_semantics=("arbitrary",),
                                             vmem_limit_bytes=VMEM_LIMIT),
        name="rwkv_mix",
    )(rw, rw, row(mu, RWKV_SHIFT_COLS), row(w0, RWKV_WIDTH), w_up.astype(F32), row(a0, RWKV_WIDTH),
      a_up.astype(F32), row(k_k, RWKV_WIDTH), row(k_a, RWKV_WIDTH), row(r_k, RWKV_WIDTH), _segment_ones())


def _shift_rows(u, halo_u, shift):
    rolled = pltpu.roll(u, shift, axis=0)
    head = jnp.where(lax.broadcasted_iota(jnp.int32, halo_u.shape, 0) < shift,
                     pltpu.roll(halo_u, shift, axis=0), rolled[0:SUBLANES])
    return jnp.concatenate([head, rolled[SUBLANES:]], axis=0)


def _out_kernel(x_ref, att_ref, gate_ref, halo_ref, y_ref, bonus_ref, cw_ref, lng_ref, lnb_ref, seg_ref,
                wout_ref, fg_ref, o_ref, *, final):
    i = pl.program_id(1)
    c0 = ATT_WIDTH
    cols = lambda ref, rows, j, width=CONV_WIDTH: ref[0, rows, c0 + j * CONV_WIDTH:c0 + j * CONV_WIDTH + width].astype(F32)
    every = slice(None)
    cb, cc, ch, zc = (cols(gate_ref, every, j) for j in range(4))
    zr = cols(gate_ref, every, 4, RWKV_WIDTH)
    u = cc * ch
    halo_u = (cols(halo_ref, every, 1) * cols(halo_ref, every, 2))[HALO_ROWS - SUBLANES:HALO_ROWS]
    halo_u = jnp.where(i > 0, halo_u, 0.0)
    cw = cw_ref[...]
    conv = (cw[0:1] * _shift_rows(u, halo_u, 2) + cw[1:2] * _shift_rows(u, halo_u, 1)) + cw[2:3] * u
    cv = (cb * conv) * (zc * _sigmoid(zc))

    seg = seg_ref[...]
    y = y_ref[0].astype(F32)
    inv_n = 1.0 / RWKV_HEAD
    mean = _seg_sum(y, seg) * inv_n
    d = y - mean
    var = _seg_sum(d * d, seg) * inv_n
    yn = d * lax.rsqrt(var + GN_EPS) * lng_ref[...] + lnb_ref[...]
    rw = (yn + bonus_ref[0].astype(F32)) * (zr * _sigmoid(zr))

    upd = (_dot(att_ref[0], wout_ref[0:ATT_WIDTH, :])
           + _dot(cv.astype(BF16), wout_ref[ATT_WIDTH:ATT_WIDTH + CONV_WIDTH, :])
           + _dot(rw.astype(BF16), wout_ref[ATT_WIDTH + CONV_WIDTH:, :]))
    xn = x_ref[0] + upd
    if final:
        ms = jnp.mean(xn * xn, axis=-1, keepdims=True)
        xn = xn * lax.rsqrt(ms + NORM_EPS) * fg_ref[...]
    o_ref[0] = xn


def _out_proj(x, att, gates, y, bonus, conv_w_l, lnx_g_l, lnx_b_l, w_out_bf16, final_g, final):
    b, s, _ = x.shape
    tm = min(OUT_TILE, s)
    row = lambda a, n: a.reshape(1, n).astype(F32)
    full = lambda shape: pl.BlockSpec(shape, lambda bi, i: (0,) * len(shape))
    tok = lambda n: pl.BlockSpec((1, tm, n), lambda bi, i: (bi, i, 0))
    halo_blocks = tm // HALO_ROWS
    return pl.pallas_call(
        functools.partial(_out_kernel, final=final),
        grid=(b, s // tm),
        in_specs=[tok(D_MODEL), tok(ATT_WIDTH), tok(GATE_COLS),
                  pl.BlockSpec((1, HALO_ROWS, GATE_COLS),
                               lambda bi, i: (bi, jnp.maximum(i * halo_blocks - 1, 0), 0)),
                  tok(RWKV_WIDTH), tok(RWKV_WIDTH),
                  full((CONV_K, CONV_WIDTH)), full((1, RWKV_WIDTH)), full((1, RWKV_WIDTH)),
                  full((RWKV_WIDTH, RWKV_WIDTH)), full((D_MODEL, D_MODEL)), full((1, D_MODEL))],
        out_specs=tok(D_MODEL),
        out_shape=jax.ShapeDtypeStruct((b, s, D_MODEL), F32),
        compiler_params=pltpu.CompilerParams(dimension_semantics=("arbitrary", "arbitrary"),
                                             vmem_limit_bytes=VMEM_LIMIT),
        name="out_proj",
    )(x, att, gates, gates, y, bonus, conv_w_l.astype(F32),
      row(lnx_g_l, RWKV_WIDTH), row(lnx_b_l, RWKV_WIDTH), _segment_ones(), w_out_bf16,
      row(final_g, D_MODEL))


def kernel(x, norm_g, w_in, w_out, final_norm_g, rel_bias, lam_qk, subln_g, conv_w, rwkv_mu, w0, w_up,
           a0, a_up, k_k, k_a, r_k, lnx_g, lnx_b):
    b, s, _ = x.shape
    tile = min(ATT_TILE, s)
    bias = _bias_tiles(rel_bias, tile)
    w_in_b = w_in.astype(BF16)
    w_out_b = w_out.astype(BF16)
    x = x.astype(F32)
    for l in range(DEPTH):
        lambda_init = 0.8 - 0.6 * math.exp(-0.3 * l)
        qkv, gates, rw = _in_proj(x.reshape(b * s, D_MODEL), norm_g[l].astype(F32), w_in_b[l])
        qkv = qkv.reshape(b, s, QKV_COLS)
        gates = gates.reshape(b, s, GATE_COLS)
        rw = rw.reshape(b, s, RWKV_SHIFT_COLS)
        att = _attention(qkv, gates, bias, lam_qk[l].astype(F32), subln_g[l].astype(F32), lambda_init)
        y, bonus = _rwkv_mix(rw, rwkv_mu[l], w0[l], w_up[l], a0[l], a_up[l], k_k[l], k_a[l], r_k[l])
        x = _out_proj(x, att, gates, y, bonus, conv_w[l], lnx_g[l], lnx_b[l], w_out_b[l],
                      final_norm_g, final=(l == DEPTH - 1))
    return x
```
